```python
import math
import jax, jax.numpy as jnp
from jax import lax
import numpy as np

D_MODEL = 1024
BATCH = 32
SEQ = 2048
DEPTH = 2

N_META = 16
EPS = 1e-6
A_HEADS = 4
A_DK = 128
A_DV = 128
A_CONV = 4
A_CHUNK = 64
A_WK = A_HEADS * A_DK
A_WV = A_HEADS * A_DV
B_HEADS = 4
B_DK = 128
B_DV = 128
B_CHUNK = 16
B_WK = B_HEADS * B_DK
B_WV = B_HEADS * B_DV
PROJ_SPLITS = (A_WK, A_WK, A_WV, A_HEADS, A_HEADS, A_WV, B_WK, B_WK, B_WV, B_WV, D_MODEL, D_MODEL)
PROJ_WIDTH = sum(PROJ_SPLITS)

kernel_name = 'hybrid_gdn_hgrn2_gated_merge'


def _rmsnorm(x, w):
    xf = x.astype(jnp.float32)
    y = xf * lax.rsqrt(jnp.mean(xf * xf, axis=-1, keepdims=True) + EPS)
    return (y * w.astype(jnp.float32)).astype(x.dtype)


def _l2norm(x):
    xf = x.astype(jnp.float32)
    return xf * lax.rsqrt(jnp.sum(xf * xf, axis=-1, keepdims=True) + EPS)


def _causal_conv(x, w):
    k, c = w.shape
    return lax.conv_general_dilated(
        x, w[:, None, :].astype(x.dtype), window_strides=(1,), padding=[(k - 1, 0)],
        dimension_numbers=('NWC', 'WIO', 'NWC'), feature_group_count=c)


def _masked_exp(diff, mask):
    return jnp.where(mask, jnp.exp(jnp.where(mask, diff, 0.0)), 0.0)


def _chunked_scan(step, state0, inputs, chunk):
    meta = tuple(a[:, :N_META] for a in inputs)
    real = tuple(a[:, N_META:] for a in inputs)
    state, out_meta = step(state0, meta)
    b, s = real[0].shape[:2]
    n = s // chunk
    xs = tuple(jnp.moveaxis(a.reshape(b, n, chunk, *a.shape[2:]), 1, 0) for a in real)
    _, out_real = lax.scan(step, state, xs)
    out_real = jnp.moveaxis(out_real, 0, 1).reshape(b, s, *out_real.shape[3:])
    return jnp.concatenate([out_meta, out_real], axis=1)


def _gdn_step(S, inp):
    q, k, v, beta, g = inp
    q, k, v = (jnp.swapaxes(a, 1, 2) for a in (q, k, v))
    beta, g = jnp.swapaxes(beta, 1, 2), jnp.swapaxes(g, 1, 2)
    L = q.shape[2]
    causal = jnp.tril(jnp.ones((L, L), dtype=bool))
    strict = jnp.tril(jnp.ones((L, L), dtype=bool), k=-1)
    G = jnp.cumsum(g, axis=-1)
    decay = _masked_exp(G[..., :, None] - G[..., None, :], causal)
    kb = k * beta[..., None]
    l_mat = jnp.where(strict, jnp.einsum('bhik,bhjk->bhij', kb, k) * decay, 0.0)
    t_mat = l_mat + jnp.eye(L, dtype=l_mat.dtype)
    rhs = jnp.concatenate([v * beta[..., None], kb * jnp.exp(G)[..., None]], axis=-1)
    sol = lax.linalg.triangular_solve(t_mat, rhs, left_side=True, lower=True, unit_diagonal=True)
    dv = v.shape[-1]
    u, w = sol[..., :dv], sol[..., dv:]
    v_new = u - jnp.einsum('bhik,bhkv->bhiv', w, S)
    scores = jnp.einsum('bhik,bhjk->bhij', q, k) * decay
    o = (jnp.einsum('bhik,bhkv->bhiv', q * jnp.exp(G)[..., None], S)
         + jnp.einsum('bhij,bhjv->bhiv', scores, v_new))
    g_last = G[..., -1:]
    S = (S * jnp.exp(g_last)[..., None]
         + jnp.einsum('bhjk,bhjv->bhkv', k * jnp.exp(g_last - G)[..., None], v_new))
    return S, jnp.swapaxes(o, 1, 2)


def _hgrn2_step(S, inp):
    q, k, v, log_f = (jnp.swapaxes(a, 1, 2) for a in inp)
    L = q.shape[2]
    causal = jnp.tril(jnp.ones((L, L), dtype=bool))
    Bc = jnp.cumsum(log_f, axis=2)
    pair = _masked_exp(Bc[:, :, :, None, :] - Bc[:, :, None, :, :],
                       causal[:, :, None])
    scores = jnp.einsum('bhik,bhjk,bhijk->bhij', q, k, pair)
    o = (jnp.einsum('bhik,bhkv->bhiv', q * jnp.exp(Bc), S)
         + jnp.einsum('bhij,bhjv->bhiv', scores, v))
    b_last = Bc[:, :, -1:]
    S = (S * jnp.exp(b_last[:, :, 0])[..., None]
         + jnp.einsum('bhjk,bhjv->bhkv', k * jnp.exp(b_last - Bc), v))
    return S, jnp.swapaxes(o, 1, 2)


def _layer(h, norm_w, w_in, conv_w, a_log, dt_bias, gnorm_a, gnorm_b, lb,
           w_branch_a, w_branch_b, w_out):
    b, t, _ = h.shape
    f32 = jnp.float32
    xn = _rmsnorm(h, norm_w)
    proj = xn @ w_in.astype(h.dtype)
    offsets = np.cumsum(PROJ_SPLITS)[:-1].tolist()
    (a_q, a_k, a_v, a_beta, a_alpha, a_z,
     b_q, b_f, b_i, b_g, gate_a, gate_b) = jnp.split(proj, offsets, axis=-1)

    qkv = jax.nn.silu(_causal_conv(jnp.concatenate([a_q, a_k, a_v], axis=-1), conv_w))
    q, k, v = jnp.split(qkv, [A_WK, 2 * A_WK], axis=-1)
    q = _l2norm(q.reshape(b, t, A_HEADS, A_DK)) * (A_DK ** -0.5)
    k = _l2norm(k.reshape(b, t, A_HEADS, A_DK))
    v = v.reshape(b, t, A_HEADS, A_DV).astype(f32)
    beta = jax.nn.sigmoid(a_beta.astype(f32))
    g = -jnp.exp(a_log.astype(f32)) * jax.nn.softplus(a_alpha.astype(f32) + dt_bias.astype(f32))
    s0_a = jnp.zeros((b, A_HEADS, A_DK, A_DV), f32)
    o_a = _chunked_scan(_gdn_step, s0_a, (q, k, v, beta, g), A_CHUNK)
    y_a = _rmsnorm(o_a, gnorm_a) * jax.nn.silu(a_z.astype(f32).reshape(b, t, A_HEADS, A_DV))
    y_a = y_a.reshape(b, t, A_WV).astype(h.dtype)

    qb = (jax.nn.silu(b_q.astype(f32)) * (B_DK ** -0.5)).reshape(b, t, B_HEADS, B_DK)
    lbf = lb.astype(f32)
    pos = lbf > 0.0
    log_sig = jax.nn.log_sigmoid(b_f.astype(f32))
    log_f = jnp.where(pos,
                      jnp.logaddexp(jnp.log(jnp.where(pos, lbf, 1.0)), jnp.log1p(-lbf) + log_sig),
                      log_sig)
    kb_ = -jnp.expm1(log_f)
    log_f = log_f.reshape(b, t, B_HEADS, B_DK)
    kb_ = kb_.reshape(b, t, B_HEADS, B_DK)
    vb = b_i.astype(f32).reshape(b, t, B_HEADS, B_DV)
    s0_b = jnp.zeros((b, B_HEADS, B_DK, B_DV), f32)
    o_b = _chunked_scan(_hgrn2_step, s0_b, (qb, kb_, vb, log_f), B_CHUNK)
    y_b = _rmsnorm(o_b, gnorm_b) * jax.nn.silu(b_g.astype(f32).reshape(b, t, B_HEADS, B_DV))
    y_b = y_b.reshape(b, t, B_WV).astype(h.dtype)

    y_a = y_a @ w_branch_a.astype(h.dtype)
    y_b = y_b @ w_branch_b.astype(h.dtype)
    mixed = jax.nn.sigmoid(gate_a) * y_a + jax.nn.sigmoid(gate_b) * y_b
    return h + mixed @ w_out.astype(h.dtype)


def setup_inputs(seed: int = 0) -> dict:
    key = jax.random.key(seed)
    ks = jax.random.split(key, 15)
    nrm = jax.random.normal
    x = nrm(ks[0], (BATCH, SEQ, D_MODEL), jnp.float32)
    meta_tokens = nrm(ks[1], (N_META, D_MODEL), jnp.float32)
    norm_w = 1.0 + 0.02 * nrm(ks[2], (DEPTH, D_MODEL), jnp.float32)
    w_in = nrm(ks[3], (DEPTH, D_MODEL, PROJ_WIDTH), jnp.float32) * D_MODEL ** -0.5
    conv_w = nrm(ks[4], (DEPTH, A_CONV, 2 * A_WK + A_WV), jnp.float32) * A_CONV ** -0.5
    a_log = jnp.log(jax.random.uniform(ks[5], (DEPTH, A_HEADS), jnp.float32, 1.0, 16.0))
    dt = jnp.exp(jax.random.uniform(ks[6], (DEPTH, A_HEADS), jnp.float32,
                                    math.log(1e-3), math.log(1e-1)))
    dt_bias = dt + jnp.log(-jnp.expm1(-dt))
    gnorm_a = 1.0 + 0.02 * nrm(ks[7], (DEPTH, A_DV), jnp.float32)
    gnorm_b = 1.0 + 0.02 * nrm(ks[8], (DEPTH, B_DV), jnp.float32)
    hgrn_lower_bounds = 0.1 * nrm(ks[9], (DEPTH, B_WK), jnp.float32)
    w_branch_a = nrm(ks[10], (DEPTH, A_WV, D_MODEL), jnp.float32) * A_WV ** -0.5
    w_branch_b = nrm(ks[11], (DEPTH, B_WV, D_MODEL), jnp.float32) * B_WV ** -0.5
    w_out = nrm(ks[12], (DEPTH, D_MODEL, D_MODEL), jnp.float32) * D_MODEL ** -0.5
    final_norm_w = 1.0 + 0.02 * nrm(ks[13], (D_MODEL,), jnp.float32)
    return {'x': x, 'meta_tokens': meta_tokens, 'norm_w': norm_w, 'w_in': w_in,
            'conv_w': conv_w, 'a_log': a_log, 'dt_bias': dt_bias, 'gnorm_a': gnorm_a,
            'gnorm_b': gnorm_b, 'hgrn_lower_bounds': hgrn_lower_bounds,
            'w_branch_a': w_branch_a, 'w_branch_b': w_branch_b, 'w_out': w_out,
            'final_norm_w': final_norm_w}


def reference(x, meta_tokens, norm_w, w_in, conv_w, a_log, dt_bias, gnorm_a, gnorm_b,
              hgrn_lower_bounds, w_branch_a, w_branch_b, w_out, final_norm_w):
    b = x.shape[0]
    meta = jnp.broadcast_to(meta_tokens.astype(x.dtype)[None], (b, N_META, D_MODEL))
    h = jnp.concatenate([meta, x], axis=1)
    lb_sm = jax.nn.softmax(hgrn_lower_bounds.astype(jnp.float32), axis=0)
    lb_all = jnp.cumsum(lb_sm, axis=0) - lb_sm[0]
    for l in range(DEPTH):
        h = _layer(h, norm_w[l], w_in[l], conv_w[l], a_log[l], dt_bias[l], gnorm_a[l],
                   gnorm_b[l], lb_all[l], w_branch_a[l], w_branch_b[l], w_out[l])
    return _rmsnorm(h, final_norm_w)[:, N_META:]
```

```python
import functools

import jax
import jax.numpy as jnp
from jax import lax
from jax.experimental import pallas as pl
from jax.experimental.pallas import tpu as pltpu

N_META = 16
EPS = 1e-6
HEADS = 4
DH = 128
WK = HEADS * DH
CONV_K = 4
CHUNK = 64
PAIR = 8
SUBLANES = 8
MAX_TILE_ROWS = 256

QKV_W = 3 * WK
OFF_AZ = 0
OFF_BQ = OFF_AZ + WK
OFF_BF = OFF_BQ + WK
OFF_BI = OFF_BF + WK
OFF_BG = OFF_BI + WK
OFF_GA = OFF_BG + WK


def _bf16(x):
    return x.astype(jnp.bfloat16)


def _dot(a, b, dims=(((1,), (0,)), ((), ())), precision=None):
    return lax.dot_general(a, b, dims, precision=precision, preferred_element_type=jnp.float32)


_NT = (((1,), (1,)), ((), ()))
_TN = (((0,), (0,)), ((), ()))


def _dot_exact_rhs(a_bf16, b_f32):
    hi = _bf16(b_f32)
    r1 = b_f32 - hi.astype(jnp.float32)
    mid = _bf16(r1)
    lo = _bf16(r1 - mid.astype(jnp.float32))
    return _dot(a_bf16, hi) + _dot(a_bf16, mid) + _dot(a_bf16, lo)


def _softplus(x):
    return jnp.maximum(x, 0.0) + jnp.log1p(jnp.exp(-jnp.abs(x)))


def _silu(x):
    return x * jax.nn.sigmoid(x)


def _cumsum_rows(x):
    n = x.shape[0]
    row = lax.broadcasted_iota(jnp.int32, x.shape, 0)
    s = 1
    while s < n:
        x = x + jnp.where(row >= s, pltpu.roll(x, s, 0), 0.0)
        s *= 2
    return x


def _layer_kernel(*refs, tile_rows, d_model, depth, layer, emit_state, final_norm):
    (h_ref, normw_ref, win_ref, convw_ref, negexp_ref, dtb_ref, gna_ref, gnb_ref, hlb_ref,
     wba_ref, wbb_ref, wout_ref, fnw_ref, sa0_ref, sb0_ref, conv0_ref) = refs[:16]
    if emit_state:
        out_ref, sa_out_ref, sb_out_ref, conv_out_ref = refs[16:20]
        scratch = refs[20:]
    else:
        out_ref = refs[16]
        scratch = refs[17:]
    proj_ref, convbuf, sa_ref, sb_ref, ya_ref, yb_ref = scratch

    f32 = jnp.float32
    tt = tile_rows
    width_rest = proj_ref.shape[1]
    off_gb = OFF_GA + d_model
    off_sm = off_gb + d_model

    @pl.when(pl.program_id(1) == 0)
    def _():
        sa_ref[...] = sa0_ref[...]
        sb_ref[...] = sb0_ref[...]
        convbuf[0:SUBLANES, :] = conv0_ref[...]

    h = h_ref[...]
    xn = h * lax.rsqrt(jnp.mean(h * h, axis=-1, keepdims=True) + EPS) * normw_ref[...]
    xn = _bf16(xn)
    convbuf[SUBLANES:SUBLANES + tt, :] = _dot(xn, win_ref[:, 0:QKV_W])
    col = 0
    while col < width_rest:
        nxt = min(col + 1024, width_rest)
        proj_ref[:, col:nxt] = _dot(xn, win_ref[:, QKV_W + col:QKV_W + nxt])
        col = nxt

    hlb_rows = [hlb_ref[i:i + 1, :] for i in range(depth)]
    hmax = functools.reduce(jnp.maximum, hlb_rows)
    hexp = [jnp.exp(r - hmax) for r in hlb_rows]
    hsum = functools.reduce(lambda a, b: a + b, hexp)
    lb = jnp.zeros_like(hmax)
    for i in range(1, layer + 1):
        lb = lb + hexp[i] / hsum

    L = CHUNK
    row = lax.broadcasted_iota(jnp.int32, (L, L), 0)
    colj = lax.broadcasted_iota(jnp.int32, (L, L), 1)
    causal = row >= colj
    strict = row > colj
    eye = jnp.where(row == colj, 1.0, 0.0).astype(f32)
    tril_bf = _bf16(jnp.where(causal, 1.0, 0.0))
    row128 = lax.broadcasted_iota(jnp.int32, (L, DH), 0)
    lane128 = lax.broadcasted_iota(jnp.int32, (L, DH), 1)
    dec_sel = (lane128 < L) & (row128 > lane128)
    convw = convw_ref[...]
    negexp = negexp_ref[...]
    dtb = dtb_ref[...]
    gna = gna_ref[...]
    gnb = gnb_ref[...]
    hi_prec = lax.Precision.HIGHEST

    def conv_silu(r0, c0):
        acc = convbuf[r0 + SUBLANES:r0 + SUBLANES + L, c0:c0 + DH] * convw[CONV_K - 1:CONV_K, c0:c0 + DH]
        for j in range(1, CONV_K):
            acc = acc + (convbuf[r0 + SUBLANES - j:r0 + SUBLANES - j + L, c0:c0 + DH]
                         * convw[CONV_K - 1 - j:CONV_K - j, c0:c0 + DH])
        return _silu(acc)

    for c in range(tt // L):
        r0 = c * L
        small = proj_ref[r0:r0 + L, off_sm:off_sm + DH]
        beta_all = jax.nn.sigmoid(small)
        g_all = negexp * _softplus(small + dtb)

        rhs = []
        for hd in range(HEADS):
            g_b = jnp.broadcast_to(g_all[:, HEADS + hd:HEADS + hd + 1], (L, DH))
            rhs.append(jnp.where(dec_sel, g_b, 0.0))
            rhs.append(g_b)
        dec = _dot_exact_rhs(tril_bf, jnp.concatenate(rhs, axis=1))

        for hd in range(HEADS):
            qc = conv_silu(r0, hd * DH)
            kc = conv_silu(r0, WK + hd * DH)
            v = conv_silu(r0, 2 * WK + hd * DH)
            q = qc * lax.rsqrt(jnp.sum(qc * qc, axis=-1, keepdims=True) + EPS) * (DH ** -0.5)
            k = kc * lax.rsqrt(jnp.sum(kc * kc, axis=-1, keepdims=True) + EPS)
            beta = jnp.broadcast_to(beta_all[:, hd:hd + 1], (L, DH))
            dlog = dec[:, 2 * hd * DH:2 * hd * DH + L]
            gb = dec[:, (2 * hd + 1) * DH:(2 * hd + 2) * DH]
            decay = jnp.exp(dlog)
            eg = jnp.exp(gb)
            g_last = gb[L - 1:L, :]
            eg_last = jnp.exp(g_last)
            eg_rev = jnp.exp(g_last - gb)

            kb = k * beta
            k_bf = _bf16(k)
            lmat = jnp.where(strict, _dot(_bf16(kb), k_bf, _NT) * decay, 0.0)
            x = -lmat
            inv = eye + x
            p = 2
            while p < L:
                x = _dot(x, x, precision=hi_prec)
                inv = inv + _dot(inv, x, precision=hi_prec)
                p *= 2
            sol = _dot(inv, jnp.concatenate([v * beta, kb * eg], axis=1), precision=hi_prec)
            u, w = sol[:, :DH], sol[:, DH:]

            s_bf = _bf16(sa_ref[hd])
            v_new = u - _dot(_bf16(w), s_bf)
            v_new_bf = _bf16(v_new)
            scores = jnp.where(causal, _dot(_bf16(q), k_bf, _NT) * decay, 0.0)
            o = _dot(_bf16(q * eg), s_bf) + _dot(_bf16(scores), v_new_bf)
            sa_ref[hd] = sa_ref[hd] * eg_last + _dot(_bf16(k * eg_rev), v_new_bf, _TN)

            z = proj_ref[r0:r0 + L, OFF_AZ + hd * DH:OFF_AZ + (hd + 1) * DH]
            y = o * lax.rsqrt(jnp.mean(o * o, axis=-1, keepdims=True) + EPS) * gna * _silu(z)
            ya_ref[r0:r0 + L, hd * DH:(hd + 1) * DH] = _bf16(y)

        for hd in range(HEADS):
            cs = slice(hd * DH, (hd + 1) * DH)
            bq = proj_ref[r0:r0 + L, OFF_BQ + hd * DH:OFF_BQ + (hd + 1) * DH]
            zf = proj_ref[r0:r0 + L, OFF_BF + hd * DH:OFF_BF + (hd + 1) * DH]
            v = proj_ref[r0:r0 + L, OFF_BI + hd * DH:OFF_BI + (hd + 1) * DH]
            q = _silu(bq) * (DH ** -0.5)
            lbh = lb[:, cs]
            pos = lbh > 0.0
            log_sig = -_softplus(-zf)
            la = jnp.log(jnp.where(pos, lbh, 1.0))
            lc = jnp.log1p(-lbh) + log_sig
            lae = jnp.maximum(la, lc) + jnp.log1p(jnp.exp(-jnp.abs(la - lc)))
            log_f = jnp.where(pos, lae, log_sig)
            k = jnp.where(pos, 1.0 - lbh, 1.0) * jax.nn.sigmoid(-zf)

            bc = _cumsum_rows(log_f)
            bc_last = bc[L - 1:L, :]
            st = sb_ref[hd]
            o = _dot(_bf16(q * jnp.exp(bc)), _bf16(st), _NT)

            for d in range(PAIR):
                valid = (row128 & (PAIR - 1)) >= d
                if d == 0:
                    prod = q * k
                    vd = v
                else:
                    e = jnp.exp(jnp.where(valid, bc - pltpu.roll(bc, d, 0), 0.0))
                    prod = jnp.where(valid, q * pltpu.roll(k, d, 0) * e, 0.0)
                    vd = pltpu.roll(v, d, 0)
                o = o + jnp.sum(prod, axis=-1, keepdims=True) * vd

            pmat = jnp.zeros((L, L), f32)
            half = PAIR
            while half < L:
                blk = 2 * half
                ref_rows = []
                for b0 in range(0, L, blk):
                    ref_rows.append(jnp.broadcast_to(bc[b0 + half - 1:b0 + half, :], (blk, DH)))
                ref = jnp.concatenate(ref_rows, axis=0) if len(ref_rows) > 1 else ref_rows[0]
                zf_ = jnp.exp(-jnp.abs(bc - ref))
                second = (row128 & (blk - 1)) >= half
                qs = jnp.where(second, q * zf_, 0.0)
                ks = jnp.where(second, 0.0, k * zf_)
                pl_ = _dot(_bf16(qs), _bf16(ks), _NT)
                if blk < L:
                    pl_ = jnp.where((row & -blk) == (colj & -blk), pl_, 0.0)
                pmat = pmat + pl_
                half = blk
            o = o + _dot(_bf16(pmat), _bf16(v))

            kd = k * jnp.exp(bc_last - bc)
            sb_ref[hd] = st * jnp.exp(bc_last) + _dot(_bf16(v), _bf16(kd), _TN)

            zg = proj_ref[r0:r0 + L, OFF_BG + hd * DH:OFF_BG + (hd + 1) * DH]
            y = o * lax.rsqrt(jnp.mean(o * o, axis=-1, keepdims=True) + EPS) * gnb * _silu(zg)
            yb_ref[r0:r0 + L, cs] = _bf16(y)

    convbuf[0:SUBLANES, :] = convbuf[tt:tt + SUBLANES, :]

    ya = _dot(ya_ref[...], wba_ref[...])
    yb = _dot(yb_ref[...], wbb_ref[...])
    mixed = (jax.nn.sigmoid(proj_ref[:, OFF_GA:OFF_GA + d_model]) * ya
             + jax.nn.sigmoid(proj_ref[:, off_gb:off_gb + d_model]) * yb)
    out = h_ref[...] + _dot(_bf16(mixed), wout_ref[...])
    if final_norm:
        out = out * lax.rsqrt(jnp.mean(out * out, axis=-1, keepdims=True) + EPS) * fnw_ref[...]
    out_ref[...] = out

    if emit_state:
        sa_out_ref[...] = sa_ref[...]
        sb_out_ref[...] = sb_ref[...]
        conv_out_ref[...] = convbuf[0:SUBLANES, :]


def _layer(h, p, layer, depth, states, *, emit_state, final_norm):
    b, t, d = h.shape
    tt = min(t, MAX_TILE_ROWS)
    assert t % tt == 0 and tt % CHUNK == 0
    width_rest = p["w_in"].shape[1] - QKV_W
    f32 = jnp.float32

    def const(shape):
        return pl.BlockSpec(shape, lambda i, j: (0,) * len(shape))

    in_specs = [
        pl.BlockSpec((None, tt, d), lambda i, j: (i, j, 0)),
        const((1, d)),
        const(p["w_in"].shape),
        const((SUBLANES, QKV_W)),
        const((1, DH)), const((1, DH)), const((1, DH)), const((1, DH)),
        const((depth, WK)),
        const((WK, d)), const((WK, d)), const((d, d)),
        const((1, d)),
        const((HEADS, DH, DH)), const((HEADS, DH, DH)), const((SUBLANES, QKV_W)),
    ]
    out_shape = [jax.ShapeDtypeStruct((b, t, d), f32)]
    out_specs = [pl.BlockSpec((None, tt, d), lambda i, j: (i, j, 0))]
    if emit_state:
        assert b == 1 and t == tt
        out_shape += [jax.ShapeDtypeStruct((HEADS, DH, DH), f32), jax.ShapeDtypeStruct((HEADS, DH, DH), f32),
                      jax.ShapeDtypeStruct((SUBLANES, QKV_W), f32)]
        out_specs += [const((HEADS, DH, DH)), const((HEADS, DH, DH)), const((SUBLANES, QKV_W))]
    scratch = [
        pltpu.VMEM((tt, width_rest), f32),
        pltpu.VMEM((tt + SUBLANES, QKV_W), f32),
        pltpu.VMEM((HEADS, DH, DH), f32),
        pltpu.VMEM((HEADS, DH, DH), f32),
        pltpu.VMEM((tt, WK), jnp.bfloat16),
        pltpu.VMEM((tt, WK), jnp.bfloat16),
    ]
    kern = functools.partial(_layer_kernel, tile_rows=tt, d_model=d, depth=depth, layer=layer,
                             emit_state=emit_state, final_norm=final_norm)
    outs = pl.pallas_call(
        kern,
        grid=(b, t // tt),
        in_specs=in_specs,
        out_specs=out_specs,
        out_shape=out_shape,
        scratch_shapes=scratch,
        compiler_params=pltpu.CompilerParams(
            dimension_semantics=("arbitrary", "arbitrary"),
            vmem_limit_bytes=56 * 1024 * 1024),
        name=f"layer{layer}_{'meta' if emit_state else 'main'}",
    )(h, p["norm_w"], p["w_in"], p["conv_w"], p["negexp"], p["dtb"], p["gna"], p["gnb"], p["hlb"],
      p["wba"], p["wbb"], p["wout"], p["fnw"], *states)
    return outs


def _prep_layer(l, norm_w, w_in, conv_w, a_log, dt_bias, gnorm_a, gnorm_b, hlb, w_branch_a, w_branch_b,
                w_out, final_norm_w):
    d = w_in.shape[1]
    f32 = jnp.float32
    wl = w_in[l]
    o = [0]
    for s in (WK, WK, WK, HEADS, HEADS, WK, WK, WK, WK, WK, d, d):
        o.append(o[-1] + s)
    small = jnp.pad(wl[:, o[3]:o[5]], ((0, 0), (0, DH - 2 * HEADS)))
    w_re = jnp.concatenate([wl[:, o[0]:o[3]], wl[:, o[5]:o[12]], small], axis=1)
    lane_pad = (HEADS, DH - 2 * HEADS)
    return {
        "norm_w": norm_w[l][None].astype(f32),
        "w_in": w_re.astype(jnp.bfloat16),
        "conv_w": jnp.pad(conv_w[l].astype(f32), ((0, SUBLANES - CONV_K), (0, 0))),
        "negexp": jnp.pad(-jnp.exp(a_log[l].astype(f32)), lane_pad)[None],
        "dtb": jnp.pad(dt_bias[l].astype(f32), lane_pad)[None],
        "gna": gnorm_a[l][None].astype(f32),
        "gnb": gnorm_b[l][None].astype(f32),
        "hlb": hlb.astype(f32),
        "wba": w_branch_a[l].astype(jnp.bfloat16),
        "wbb": w_branch_b[l].astype(jnp.bfloat16),
        "wout": w_out[l].astype(jnp.bfloat16),
        "fnw": final_norm_w[None].astype(f32),
    }


def kernel(x, meta_tokens, norm_w, w_in, conv_w, a_log, dt_bias, gnorm_a, gnorm_b, hgrn_lower_bounds,
           w_branch_a, w_branch_b, w_out, final_norm_w):
    depth = w_in.shape[0]
    f32 = jnp.float32
    assert N_META <= CHUNK
    h = x
    hm = jnp.pad(meta_tokens.astype(x.dtype), ((CHUNK - N_META, 0), (0, 0)))[None]
    zero_state = (jnp.zeros((HEADS, DH, DH), f32), jnp.zeros((HEADS, DH, DH), f32),
                  jnp.zeros((SUBLANES, QKV_W), f32))
    for l in range(depth):
        p = _prep_layer(l, norm_w, w_in, conv_w, a_log, dt_bias, gnorm_a, gnorm_b, hgrn_lower_bounds,
                        w_branch_a, w_branch_b, w_out, final_norm_w)
        hm, s_a, s_b, conv_hist = _layer(hm, p, l, depth, zero_state, emit_state=True, final_norm=False)
        (h,) = _layer(h, p, l, depth, (s_a, s_b, conv_hist), emit_state=False, final_norm=(l == depth - 1))
    return h
```

```python
import functools

import jax
import jax.numpy as jnp
from jax import lax
from jax.experimental import pallas as pl
from jax.experimental.pallas import tpu as pltpu

N_META = 16
EPS = 1e-6
HEADS = 4
DH = 128
WK = HEADS * DH
CONV_K = 4
CHUNK = 64
CATW = HEADS * CHUNK
PAIR = 8
SUBLANES = 8
MAX_TILE_ROWS = 256

QKV_W = 3 * WK
OFF_AZ = 0
OFF_BQ = OFF_AZ + WK
OFF_BF = OFF_BQ + WK
OFF_BI = OFF_BF + WK
OFF_BG = OFF_BI + WK
OFF_GA = OFF_BG + WK


def _bf16(x):
    return x.astype(jnp.bfloat16)


def _dot(a, b, dims=(((1,), (0,)), ((), ()))):
    return lax.dot_general(a, b, dims, preferred_element_type=jnp.float32)


_NT = (((1,), (1,)), ((), ()))
_TN = (((0,), (0,)), ((), ()))


def _dot_split_rhs(a_bf16, b_f32):
    hi = _bf16(b_f32)
    mid = _bf16(b_f32 - hi.astype(jnp.float32))
    return _dot(a_bf16, hi) + _dot(a_bf16, mid)


def _softplus(x):
    return jnp.maximum(x, 0.0) + jnp.log1p(jnp.exp(-jnp.abs(x)))


def _silu(x):
    return x * jax.nn.sigmoid(x)


def _cumsum_rows(x):
    n = x.shape[0]
    row = lax.broadcasted_iota(jnp.int32, x.shape, 0)
    s = 1
    while s < n:
        x = x + jnp.where(row >= s, pltpu.roll(x, s, 0), 0.0)
        s *= 2
    return x


def _block_rows(blocks):
    n = len(blocks)
    rows = []
    for i, blk in enumerate(blocks):
        zero = jnp.zeros_like(blk)
        rows.append(jnp.concatenate([blk if j == i else zero for j in range(n)], axis=1))
    return jnp.concatenate(rows, axis=0)


def _run_round_robin(tasks):
    tasks = list(tasks)
    while tasks:
        for t in list(tasks):
            try:
                next(t)
            except StopIteration:
                tasks.remove(t)


def _layer_kernel(*refs, tile_rows, d_model, depth, layer, emit_state, final_norm):
    (h_ref, normw_ref, win_ref, convw_ref, negexp_ref, dtb_ref, gna_ref, gnb_ref, hlb_ref,
     wba_ref, wbb_ref, wout_ref, fnw_ref, sa0_ref, sb0_ref, conv0_ref) = refs[:16]
    if emit_state:
        out_ref, sa_out_ref, sb_out_ref, conv_out_ref = refs[16:20]
        scratch = refs[20:]
    else:
        out_ref = refs[16]
        scratch = refs[17:]
    proj_ref, convbuf, sa_ref, sb_ref, ya_ref, yb_ref = scratch

    f32 = jnp.float32
    tt = tile_rows
    width_rest = proj_ref.shape[1]
    off_gb = OFF_GA + d_model
    off_sm = off_gb + d_model
    L = CHUNK
    n_chunks = tt // L

    @pl.when(pl.program_id(1) == 0)
    def _():
        sa_ref[...] = sa0_ref[...]
        sb_ref[...] = sb0_ref[...]
        convbuf[0:SUBLANES, :] = conv0_ref[...]

    h = h_ref[...]
    xn = h * lax.rsqrt(jnp.mean(h * h, axis=-1, keepdims=True) + EPS) * normw_ref[...]
    xn = _bf16(xn)
    convbuf[SUBLANES:SUBLANES + tt, :] = _dot(xn, win_ref[:, 0:QKV_W])
    col = 0
    while col < width_rest:
        nxt = min(col + 1024, width_rest)
        proj_ref[:, col:nxt] = _dot(xn, win_ref[:, QKV_W + col:QKV_W + nxt])
        col = nxt

    hlb_rows = [hlb_ref[i:i + 1, :] for i in range(depth)]
    hmax = functools.reduce(jnp.maximum, hlb_rows)
    hexp = [jnp.exp(r - hmax) for r in hlb_rows]
    hsum = functools.reduce(lambda a, b: a + b, hexp)
    lb = jnp.zeros_like(hmax)
    for i in range(1, layer + 1):
        lb = lb + hexp[i] / hsum

    ci = lax.broadcasted_iota(jnp.int32, (L, CATW), 0)
    cj = lax.broadcasted_iota(jnp.int32, (L, CATW), 1) & (L - 1)
    causal_cat = ci >= cj
    strict_cat = ci > cj
    eye_cat = jnp.where(ci == cj, 1.0, 0.0).astype(f32)
    pair_cat = (ci & -PAIR) == (cj & -PAIR)
    bi = lax.broadcasted_iota(jnp.int32, (CATW, CATW), 0)
    bj = lax.broadcasted_iota(jnp.int32, (CATW, CATW), 1)
    bd_mask = (bi & -L) == (bj & -L)
    row = lax.broadcasted_iota(jnp.int32, (L, L), 0)
    colj = lax.broadcasted_iota(jnp.int32, (L, L), 1)
    tril_bf = _bf16(jnp.where(row >= colj, 1.0, 0.0))
    row128 = lax.broadcasted_iota(jnp.int32, (L, DH), 0)
    lane128 = lax.broadcasted_iota(jnp.int32, (L, DH), 1)
    convw = convw_ref[...]
    negexp = negexp_ref[...]
    dtb = dtb_ref[...]
    gna = gna_ref[...]
    gnb = gnb_ref[...]

    def block_diag(cat):
        return _bf16(jnp.where(bd_mask, jnp.concatenate([cat] * HEADS, axis=0), 0.0))

    def conv_silu(r0, c0):
        acc = convbuf[r0 + SUBLANES:r0 + SUBLANES + L, c0:c0 + DH] * convw[CONV_K - 1:CONV_K, c0:c0 + DH]
        for j in range(1, CONV_K):
            acc = acc + (convbuf[r0 + SUBLANES - j:r0 + SUBLANES - j + L, c0:c0 + DH]
                         * convw[CONV_K - 1 - j:CONV_K - j, c0:c0 + DH])
        return _silu(acc)

    gdn = [None] * n_chunks
    hgrn = [None] * n_chunks

    def gdn_prep(c):
        r0 = c * L
        small = proj_ref[r0:r0 + L, off_sm:off_sm + DH]
        beta_all = jax.nn.sigmoid(small)
        g_all = negexp * _softplus(small + dtb)
        beta_b = [jnp.broadcast_to(beta_all[:, hd:hd + 1], (L, DH)) for hd in range(HEADS)]
        g_b = [jnp.broadcast_to(g_all[:, HEADS + hd:HEADS + hd + 1], (L, DH)) for hd in range(HEADS)]
        g_cat = jnp.concatenate([jnp.where(lane128 < L, g_b[2 * t], g_b[2 * t + 1]) for t in range(HEADS // 2)],
                                axis=1)
        dec = _dot_split_rhs(tril_bf, jnp.concatenate([jnp.where(strict_cat, g_cat, 0.0)] + g_b, axis=1))
        q, k, v = [], [], []
        for hd in range(HEADS):
            qc = conv_silu(r0, hd * DH)
            kc = conv_silu(r0, WK + hd * DH)
            v.append(conv_silu(r0, 2 * WK + hd * DH))
            q.append(qc * lax.rsqrt(jnp.sum(qc * qc, axis=-1, keepdims=True) + EPS) * (DH ** -0.5))
            k.append(kc * lax.rsqrt(jnp.sum(kc * kc, axis=-1, keepdims=True) + EPS))
        yield
        decay = jnp.exp(dec[:, :CATW])
        gb = [dec[:, CATW + hd * DH:CATW + (hd + 1) * DH] for hd in range(HEADS)]
        kb = [k[hd] * beta_b[hd] for hd in range(HEADS)]
        k_bf = [_bf16(x) for x in k]
        lhs = jnp.concatenate([jnp.concatenate([_bf16(x) for x in kb], axis=1),
                               jnp.concatenate([_bf16(x) for x in q], axis=1)], axis=0)
        kkqk = _dot(lhs, _block_rows(k_bf), _NT)
        yield
        lcat = jnp.where(strict_cat, kkqk[:L] * decay, 0.0)
        scores = _bf16(jnp.where(causal_cat, kkqk[L:] * decay, 0.0))
        xb = jnp.where(pair_cat, -lcat, 0.0)
        x2 = _dot(_bf16(xb), block_diag(xb))
        yield
        bd2 = block_diag(x2)
        ipx = eye_cat + xb
        x4 = _dot(_bf16(x2), bd2)
        p1 = ipx + _dot(_bf16(ipx), bd2)
        yield
        inv = p1 + _dot(_bf16(p1), block_diag(x4))
        yield
        s = PAIR
        while s < L:
            lower_left = ((ci & -(2 * s)) == (cj & -(2 * s))) & ((ci & s) != 0) & ((cj & s) == 0)
            e = _dot(_bf16(jnp.where(lower_left, lcat, 0.0)), block_diag(inv))
            yield
            inv = inv - _dot(_bf16(inv), block_diag(e))
            yield
            s *= 2
        eg = [jnp.exp(x) for x in gb]
        rhs = [_bf16(jnp.concatenate([v[hd] * beta_b[hd], kb[hd] * eg[hd]], axis=1)) for hd in range(HEADS)]
        sol = _dot(_bf16(inv), _block_rows(rhs))
        yield
        g_last = [x[L - 1:L, :] for x in gb]
        gdn[c] = dict(
            u=[sol[:, 2 * hd * DH:(2 * hd + 1) * DH] for hd in range(HEADS)],
            wq=[_bf16(jnp.concatenate([sol[:, (2 * hd + 1) * DH:(2 * hd + 2) * DH], q[hd] * eg[hd]], axis=0))
                for hd in range(HEADS)],
            kr=[_bf16(k[hd] * jnp.exp(g_last[hd] - gb[hd])) for hd in range(HEADS)],
            eg_last=[jnp.exp(x) for x in g_last],
            scores=scores,
        )

    def hgrn_prep(c):
        r0 = c * L
        q, k, v, bc = [], [], [], []
        for hd in range(HEADS):
            cs = slice(hd * DH, (hd + 1) * DH)
            bq = proj_ref[r0:r0 + L, OFF_BQ + hd * DH:OFF_BQ + (hd + 1) * DH]
            zf = proj_ref[r0:r0 + L, OFF_BF + hd * DH:OFF_BF + (hd + 1) * DH]
            v.append(proj_ref[r0:r0 + L, OFF_BI + hd * DH:OFF_BI + (hd + 1) * DH])
            q.append(_silu(bq) * (DH ** -0.5))
            lbh = lb[:, cs]
            pos = lbh > 0.0
            log_sig = -_softplus(-zf)
            la = jnp.log(jnp.where(pos, lbh, 1.0))
            lc = jnp.log1p(-lbh) + log_sig
            lae = jnp.maximum(la, lc) + jnp.log1p(jnp.exp(-jnp.abs(la - lc)))
            log_f = jnp.where(pos, lae, log_sig)
            k.append(jnp.where(pos, 1.0 - lbh, 1.0) * jax.nn.sigmoid(-zf))
            bc.append(_cumsum_rows(log_f))
        yield
        o = []
        for hd in range(HEADS):
            acc = jnp.sum(q[hd] * k[hd], axis=-1, keepdims=True) * v[hd]
            for d in range(1, PAIR):
                valid = (row128 & (PAIR - 1)) >= d
                e = jnp.exp(jnp.where(valid, bc[hd] - pltpu.roll(bc[hd], d, 0), 0.0))
                prod = jnp.where(valid, q[hd] * pltpu.roll(k[hd], d, 0) * e, 0.0)
                acc = acc + jnp.sum(prod, axis=-1, keepdims=True) * pltpu.roll(v[hd], d, 0)
            o.append(acc)
            if hd % 2 == 1:
                yield
        pmat = [jnp.zeros((L, L), f32) for _ in range(HEADS)]
        half = PAIR
        while half < L:
            blk = 2 * half
            second = (row128 & (blk - 1)) >= half
            same_blk = (row & -blk) == (colj & -blk)
            for hd in range(HEADS):
                ref = jnp.concatenate(
                    [jnp.broadcast_to(bc[hd][b0 + half - 1:b0 + half, :], (blk, DH)) for b0 in range(0, L, blk)],
                    axis=0)
                zd = jnp.exp(-jnp.abs(bc[hd] - ref))
                qs = jnp.where(second, q[hd] * zd, 0.0)
                ks = jnp.where(second, 0.0, k[hd] * zd)
                pl_ = _dot(_bf16(qs), _bf16(ks), _NT)
                pmat[hd] = pmat[hd] + (jnp.where(same_blk, pl_, 0.0) if blk < L else pl_)
            yield
            half = blk
        v_bf = [_bf16(x) for x in v]
        o_intra = _dot(_bf16(jnp.concatenate(pmat, axis=1)), _block_rows(v_bf))
        yield
        bc_last = [x[L - 1:L, :] for x in bc]
        hgrn[c] = dict(
            o=[o[hd] + o_intra[:, hd * DH:(hd + 1) * DH] for hd in range(HEADS)],
            qd=[_bf16(q[hd] * jnp.exp(bc[hd])) for hd in range(HEADS)],
            kd=[_bf16(k[hd] * jnp.exp(bc_last[hd] - bc[hd])) for hd in range(HEADS)],
            v=v_bf,
            e_last=[jnp.exp(x) for x in bc_last],
        )

    _run_round_robin([gdn_prep(c) for c in range(n_chunks)] + [hgrn_prep(c) for c in range(n_chunks)])

    def gdn_seq():
        s = [sa_ref[hd] for hd in range(HEADS)]
        for c in range(n_chunks):
            r0 = c * L
            g = gdn[c]
            s_bf = [_bf16(x) for x in s]
            ws = [_dot(g["wq"][hd], s_bf[hd]) for hd in range(HEADS)]
            yield
            v_new = [_bf16(g["u"][hd] - ws[hd][:L]) for hd in range(HEADS)]
            s = [s[hd] * g["eg_last"][hd] + _dot(g["kr"][hd], v_new[hd], _TN) for hd in range(HEADS)]
            sv = _dot(g["scores"], _block_rows(v_new))
            yield
            for hd in range(HEADS):
                o = ws[hd][L:] + sv[:, hd * DH:(hd + 1) * DH]
                z = proj_ref[r0:r0 + L, OFF_AZ + hd * DH:OFF_AZ + (hd + 1) * DH]
                y = o * lax.rsqrt(jnp.mean(o * o, axis=-1, keepdims=True) + EPS) * gna * _silu(z)
                ya_ref[r0:r0 + L, hd * DH:(hd + 1) * DH] = _bf16(y)
        for hd in range(HEADS):
            sa_ref[hd] = s[hd]

    def hgrn_seq():
        st = [sb_ref[hd] for hd in range(HEADS)]
        for c in range(n_chunks):
            r0 = c * L
            g = hgrn[c]
            o_inter = [_dot(g["qd"][hd], _bf16(st[hd]), _NT) for hd in range(HEADS)]
            st = [st[hd] * g["e_last"][hd] + _dot(g["v"][hd], g["kd"][hd], _TN) for hd in range(HEADS)]
            yield
            for hd in range(HEADS):
                o = g["o"][hd] + o_inter[hd]
                zg = proj_ref[r0:r0 + L, OFF_BG + hd * DH:OFF_BG + (hd + 1) * DH]
                y = o * lax.rsqrt(jnp.mean(o * o, axis=-1, keepdims=True) + EPS) * gnb * _silu(zg)
                yb_ref[r0:r0 + L, hd * DH:(hd + 1) * DH] = _bf16(y)
            yield
        for hd in range(HEADS):
            sb_ref[hd] = st[hd]

    _run_round_robin([gdn_seq(), hgrn_seq()])

    convbuf[0:SUBLANES, :] = convbuf[tt:tt + SUBLANES, :]

    ya = _dot(ya_ref[...], wba_ref[...])
    yb = _dot(yb_ref[...], wbb_ref[...])
    mixed = (jax.nn.sigmoid(proj_ref[:, OFF_GA:OFF_GA + d_model]) * ya
             + jax.nn.sigmoid(proj_ref[:, off_gb:off_gb + d_model]) * yb)
    out = h_ref[...] + _dot(_bf16(mixed), wout_ref[...])
    if final_norm:
        out = out * lax.rsqrt(jnp.mean(out * out, axis=-1, keepdims=True) + EPS) * fnw_ref[...]
    out_ref[...] = out

    if emit_state:
        sa_out_ref[...] = sa_ref[...]
        sb_out_ref[...] = sb_ref[...]
        conv_out_ref[...] = convbuf[0:SUBLANES, :]


def _layer(h, p, layer, depth, states, *, emit_state, final_norm):
    b, t, d = h.shape
    tt = min(t, MAX_TILE_ROWS)
    assert t % tt == 0 and tt % CHUNK == 0
    width_rest = p["w_in"].shape[1] - QKV_W
    f32 = jnp.float32

    def const(shape):
        return pl.BlockSpec(shape, lambda i, j: (0,) * len(shape))

    in_specs = [
        pl.BlockSpec((None, tt, d), lambda i, j: (i, j, 0)),
        const((1, d)),
        const(p["w_in"].shape),
        const((SUBLANES, QKV_W)),
        const((1, DH)), const((1, DH)), const((1, DH)), const((1, DH)),
        const((depth, WK)),
        const((WK, d)), const((WK, d)), const((d, d)),
        const((1, d)),
        const((HEADS, DH, DH)), const((HEADS, DH, DH)), const((SUBLANES, QKV_W)),
    ]
    out_shape = [jax.ShapeDtypeStruct((b, t, d), f32)]
    out_specs = [pl.BlockSpec((None, tt, d), lambda i, j: (i, j, 0))]
    if emit_state:
        assert b == 1 and t == tt
        out_shape += [jax.ShapeDtypeStruct((HEADS, DH, DH), f32), jax.ShapeDtypeStruct((HEADS, DH, DH), f32),
                      jax.ShapeDtypeStruct((SUBLANES, QKV_W), f32)]
        out_specs += [const((HEADS, DH, DH)), const((HEADS, DH, DH)), const((SUBLANES, QKV_W))]
    scratch = [
        pltpu.VMEM((tt, width_rest), f32),
        pltpu.VMEM((tt + SUBLANES, QKV_W), f32),
        pltpu.VMEM((HEADS, DH, DH), f32),
        pltpu.VMEM((HEADS, DH, DH), f32),
        pltpu.VMEM((tt, WK), jnp.bfloat16),
        pltpu.VMEM((tt, WK), jnp.bfloat16),
    ]
    kern = functools.partial(_layer_kernel, tile_rows=tt, d_model=d, depth=depth, layer=layer,
                             emit_state=emit_state, final_norm=final_norm)
    outs = pl.pallas_call(
        kern,
        grid=(b, t // tt),
        in_specs=in_specs,
        out_specs=out_specs,
        out_shape=out_shape,
        scratch_shapes=scratch,
        compiler_params=pltpu.CompilerParams(
            dimension_semantics=("arbitrary", "arbitrary"),
            vmem_limit_bytes=56 * 1024 * 1024),
        name=f"layer{layer}_{'meta' if emit_state else 'main'}",
    )(h, p["norm_w"], p["w_in"], p["conv_w"], p["negexp"], p["dtb"], p["gna"], p["gnb"], p["hlb"],
      p["wba"], p["wbb"], p["wout"], p["fnw"], *states)
    return outs


def _prep_layer(l, norm_w, w_in, conv_w, a_log, dt_bias, gnorm_a, gnorm_b, hlb, w_branch_a, w_branch_b,
                w_out, final_norm_w):
    d = w_in.shape[1]
    f32 = jnp.float32
    wl = w_in[l]
    o = [0]
    for s in (WK, WK, WK, HEADS, HEADS, WK, WK, WK, WK, WK, d, d):
        o.append(o[-1] + s)
    small = jnp.pad(wl[:, o[3]:o[5]], ((0, 0), (0, DH - 2 * HEADS)))
    w_re = jnp.concatenate([wl[:, o[0]:o[3]], wl[:, o[5]:o[12]], small], axis=1)
    lane_pad = (HEADS, DH - 2 * HEADS)
    return {
        "norm_w": norm_w[l][None].astype(f32),
        "w_in": w_re.astype(jnp.bfloat16),
        "conv_w": jnp.pad(conv_w[l].astype(f32), ((0, SUBLANES - CONV_K), (0, 0))),
        "negexp": jnp.pad(-jnp.exp(a_log[l].astype(f32)), lane_pad)[None],
        "dtb": jnp.pad(dt_bias[l].astype(f32), lane_pad)[None],
        "gna": gnorm_a[l][None].astype(f32),
        "gnb": gnorm_b[l][None].astype(f32),
        "hlb": hlb.astype(f32),
        "wba": w_branch_a[l].astype(jnp.bfloat16),
        "wbb": w_branch_b[l].astype(jnp.bfloat16),
        "wout": w_out[l].astype(jnp.bfloat16),
        "fnw": final_norm_w[None].astype(f32),
    }


def kernel(x, meta_tokens, norm_w, w_in, conv_w, a_log, dt_bias, gnorm_a, gnorm_b, hgrn_lower_bounds,
           w_branch_a, w_branch_b, w_out, final_norm_w):
    depth = w_in.shape[0]
    f32 = jnp.float32
    assert N_META <= CHUNK
    h = x
    hm = jnp.pad(meta_tokens.astype(x.dtype), ((CHUNK - N_META, 0), (0, 0)))[None]
    zero_state = (jnp.zeros((HEADS, DH, DH), f32), jnp.zeros((HEADS, DH, DH), f32),
                  jnp.zeros((SUBLANES, QKV_W), f32))
    for l in range(depth):
        p = _prep_layer(l, norm_w, w_in, conv_w, a_log, dt_bias, gnorm_a, gnorm_b, hgrn_lower_bounds,
                        w_branch_a, w_branch_b, w_out, final_norm_w)
        hm, s_a, s_b, conv_hist = _layer(hm, p, l, depth, zero_state, emit_state=True, final_norm=False)
        (h,) = _layer(h, p, l, depth, (s_a, s_b, conv_hist), emit_state=False, final_norm=(l == depth - 1))
    return h
```

```python
import functools

import jax
import jax.numpy as jnp
from jax import lax
from jax.experimental import pallas as pl
from jax.experimental.pallas import tpu as pltpu

N_META = 16
EPS = 1e-6
HEADS = 4
DH = 128
WK = HEADS * DH
CONV_K = 4
CHUNK = 64
CATW = HEADS * CHUNK
PAIR = 8
SUBLANES = 8
MAX_TILE_ROWS = 256

QKV_W = 3 * WK
OFF_AZ = 0
OFF_BQ = OFF_AZ + WK
OFF_BF = OFF_BQ + WK
OFF_BI = OFF_BF + WK
OFF_BG = OFF_BI + WK
OFF_GA = OFF_BG + WK


MXU_DTYPE = jnp.bfloat16


def _bf16(x):
    return x.astype(MXU_DTYPE)


def _dot(a, b, dims=(((1,), (0,)), ((), ()))):
    return lax.dot_general(a, b, dims, preferred_element_type=jnp.float32)


_NT = (((1,), (1,)), ((), ()))
_TN = (((0,), (0,)), ((), ()))


def _dot_split_rhs(a_bf16, b_f32):
    hi = _bf16(b_f32)
    mid = _bf16(b_f32 - hi.astype(jnp.float32))
    return _dot(a_bf16, hi) + _dot(a_bf16, mid)


def _log1pexp_neg_abs(x):
    return jnp.log(1.0 + jnp.exp(-jnp.abs(x)))


def _softplus(x):
    return jnp.maximum(x, 0.0) + _log1pexp_neg_abs(x)


def _mid_rows(x, half):
    n = x.shape[0]
    blk = 2 * half
    if blk >= 2 * SUBLANES:
        return jnp.concatenate(
            [jnp.broadcast_to(x[b0 + half - 1:b0 + half, :], (blk, x.shape[1])) for b0 in range(0, n, blk)], axis=0)
    x3 = x.reshape(n // SUBLANES, SUBLANES, x.shape[1])
    sub = lax.broadcasted_iota(jnp.int32, x3.shape, 1)
    out = None
    for b0 in range(0, SUBLANES, blk):
        piece = jnp.broadcast_to(x3[:, b0 + half - 1:b0 + half, :], x3.shape)
        out = piece if out is None else jnp.where(sub >= b0, piece, out)
    return out.reshape(x.shape)


def _silu(x):
    return x * jax.nn.sigmoid(x)


def _block_rows(blocks):
    n = len(blocks)
    rows = []
    for i, blk in enumerate(blocks):
        zero = jnp.zeros_like(blk)
        rows.append(jnp.concatenate([blk if j == i else zero for j in range(n)], axis=1))
    return jnp.concatenate(rows, axis=0)


def _run_round_robin(tasks):
    tasks = list(tasks)
    while tasks:
        for t in list(tasks):
            try:
                next(t)
            except StopIteration:
                tasks.remove(t)


def _layer_kernel(*refs, tile_rows, d_model, depth, layer, emit_state, final_norm):
    (h_ref, normw_ref, win_ref, convw_ref, negexp_ref, dtb_ref, gna_ref, gnb_ref, hlb_ref,
     wba_ref, wbb_ref, wout_ref, fnw_ref, sa0_ref, sb0_ref, conv0_ref) = refs[:16]
    if emit_state:
        out_ref, sa_out_ref, sb_out_ref, conv_out_ref = refs[16:20]
        scratch = refs[20:]
    else:
        out_ref = refs[16]
        scratch = refs[17:]
    proj_ref, convbuf, sa_ref, sb_ref, ya_ref, yb_ref = scratch

    f32 = jnp.float32
    tt = tile_rows
    width_rest = proj_ref.shape[1]
    off_gb = OFF_GA + d_model
    off_sm = off_gb + d_model
    L = CHUNK
    n_chunks = tt // L

    @pl.when(pl.program_id(1) == 0)
    def _():
        sa_ref[...] = sa0_ref[...]
        sb_ref[...] = sb0_ref[...]
        convbuf[0:SUBLANES, :] = conv0_ref[...]

    h = h_ref[...]
    xn = h * lax.rsqrt(jnp.mean(h * h, axis=-1, keepdims=True) + EPS) * normw_ref[...]
    xn = _bf16(xn)
    convbuf[SUBLANES:SUBLANES + tt, :] = _dot(xn, win_ref[:, 0:QKV_W])
    col = 0
    while col < width_rest:
        nxt = min(col + 1024, width_rest)
        proj_ref[:, col:nxt] = _dot(xn, win_ref[:, QKV_W + col:QKV_W + nxt])
        col = nxt

    hlb_rows = [hlb_ref[i:i + 1, :] for i in range(depth)]
    hmax = functools.reduce(jnp.maximum, hlb_rows)
    hexp = [jnp.exp(r - hmax) for r in hlb_rows]
    hsum = functools.reduce(lambda a, b: a + b, hexp)
    lb = jnp.zeros_like(hmax)
    for i in range(1, layer + 1):
        lb = lb + hexp[i] / hsum

    ci = lax.broadcasted_iota(jnp.int32, (L, CATW), 0)
    cj = lax.broadcasted_iota(jnp.int32, (L, CATW), 1) & (L - 1)
    causal_cat = ci >= cj
    strict_cat = ci > cj
    eye_cat = jnp.where(ci == cj, 1.0, 0.0).astype(f32)
    pair_cat = (ci & -PAIR) == (cj & -PAIR)
    bi = lax.broadcasted_iota(jnp.int32, (CATW, CATW), 0)
    bj = lax.broadcasted_iota(jnp.int32, (CATW, CATW), 1)
    bd_mask = (bi & -L) == (bj & -L)
    row = lax.broadcasted_iota(jnp.int32, (L, L), 0)
    colj = lax.broadcasted_iota(jnp.int32, (L, L), 1)
    tril_bf = _bf16(jnp.where(row >= colj, 1.0, 0.0))
    row128 = lax.broadcasted_iota(jnp.int32, (L, DH), 0)
    lane128 = lax.broadcasted_iota(jnp.int32, (L, DH), 1)
    convw = convw_ref[...]
    negexp = negexp_ref[...]
    dtb = dtb_ref[...]
    gna = gna_ref[...]
    gnb = gnb_ref[...]

    def block_diag(cat):
        return _bf16(jnp.where(bd_mask, jnp.concatenate([cat] * HEADS, axis=0), 0.0))

    def conv_silu(r0, c0):
        acc = convbuf[r0 + SUBLANES:r0 + SUBLANES + L, c0:c0 + DH] * convw[CONV_K - 1:CONV_K, c0:c0 + DH]
        for j in range(1, CONV_K):
            acc = acc + (convbuf[r0 + SUBLANES - j:r0 + SUBLANES - j + L, c0:c0 + DH]
                         * convw[CONV_K - 1 - j:CONV_K - j, c0:c0 + DH])
        return _silu(acc)

    gdn = [None] * n_chunks
    hgrn = [None] * n_chunks

    def gdn_prep(c):
        r0 = c * L
        small = proj_ref[r0:r0 + L, off_sm:off_sm + DH]
        beta_all = jax.nn.sigmoid(small)
        g_all = negexp * _softplus(small + dtb)
        beta_b = [jnp.broadcast_to(beta_all[:, hd:hd + 1], (L, DH)) for hd in range(HEADS)]
        g_b = [jnp.broadcast_to(g_all[:, HEADS + hd:HEADS + hd + 1], (L, DH)) for hd in range(HEADS)]
        g_cat = jnp.concatenate([jnp.where(lane128 < L, g_b[2 * t], g_b[2 * t + 1]) for t in range(HEADS // 2)],
                                axis=1)
        dec = _dot_split_rhs(tril_bf, jnp.concatenate([jnp.where(strict_cat, g_cat, 0.0)] + g_b, axis=1))
        q, k, v = [], [], []
        for hd in range(HEADS):
            qc = conv_silu(r0, hd * DH)
            kc = conv_silu(r0, WK + hd * DH)
            v.append(conv_silu(r0, 2 * WK + hd * DH))
            q.append(qc * lax.rsqrt(jnp.sum(qc * qc, axis=-1, keepdims=True) + EPS) * (DH ** -0.5))
            k.append(kc * lax.rsqrt(jnp.sum(kc * kc, axis=-1, keepdims=True) + EPS))
        yield
        decay = jnp.exp(dec[:, :CATW])
        gb = [dec[:, CATW + hd * DH:CATW + (hd + 1) * DH] for hd in range(HEADS)]
        kb = [k[hd] * beta_b[hd] for hd in range(HEADS)]
        k_bf = [_bf16(x) for x in k]
        lhs = jnp.concatenate([jnp.concatenate([_bf16(x) for x in kb], axis=1),
                               jnp.concatenate([_bf16(x) for x in q], axis=1)], axis=0)
        kkqk = _dot(lhs, _block_rows(k_bf), _NT)
        yield
        lcat = jnp.where(strict_cat, kkqk[:L] * decay, 0.0)
        scores = _bf16(jnp.where(causal_cat, kkqk[L:] * decay, 0.0))
        xb = jnp.where(pair_cat, -lcat, 0.0)
        x2 = _dot(_bf16(xb), block_diag(xb))
        yield
        bd2 = block_diag(x2)
        ipx = eye_cat + xb
        x4 = _dot(_bf16(x2), bd2)
        p1 = ipx + _dot(_bf16(ipx), bd2)
        yield
        inv = p1 + _dot(_bf16(p1), block_diag(x4))
        yield
        s = PAIR
        while s < L:
            lower_left = ((ci & -(2 * s)) == (cj & -(2 * s))) & ((ci & s) != 0) & ((cj & s) == 0)
            e = _dot(_bf16(jnp.where(lower_left, lcat, 0.0)), block_diag(inv))
            yield
            inv = inv - _dot(_bf16(inv), block_diag(e))
            yield
            s *= 2
        eg = [jnp.exp(x) for x in gb]
        rhs = [_bf16(jnp.concatenate([v[hd] * beta_b[hd], kb[hd] * eg[hd]], axis=1)) for hd in range(HEADS)]
        sol = _dot(_bf16(inv), _block_rows(rhs))
        yield
        g_last = [x[L - 1:L, :] for x in gb]
        gdn[c] = dict(
            u=[sol[:, 2 * hd * DH:(2 * hd + 1) * DH] for hd in range(HEADS)],
            wq=[_bf16(jnp.concatenate(
                [jnp.concatenate([sol[:, (2 * hd + 1) * DH:(2 * hd + 2) * DH], q[hd] * eg[hd]], axis=0)
                 for hd in (2 * p, 2 * p + 1)], axis=1)) for p in range(HEADS // 2)],
            kr=[_bf16(k[hd] * jnp.exp(g_last[hd] - gb[hd])) for hd in range(HEADS)],
            eg_last=[jnp.exp(x) for x in g_last],
            scores=scores,
        )

    def hgrn_prep(c):
        r0 = c * L
        q, k, v, log_f = [], [], [], []
        for hd in range(HEADS):
            cs = slice(hd * DH, (hd + 1) * DH)
            bq = proj_ref[r0:r0 + L, OFF_BQ + hd * DH:OFF_BQ + (hd + 1) * DH]
            zf = proj_ref[r0:r0 + L, OFF_BF + hd * DH:OFF_BF + (hd + 1) * DH]
            v.append(proj_ref[r0:r0 + L, OFF_BI + hd * DH:OFF_BI + (hd + 1) * DH])
            q.append(_silu(bq) * (DH ** -0.5))
            log_sig = -_softplus(-zf)
            if layer == 0:
                lf = log_sig
                k.append(jax.nn.sigmoid(-zf))
            else:
                lbh = lb[:, cs]
                pos = lbh > 0.0
                la = jnp.log(jnp.where(pos, lbh, 1.0))
                lc = jnp.log1p(-lbh) + log_sig
                lae = jnp.maximum(la, lc) + _log1pexp_neg_abs(la - lc)
                lf = jnp.where(pos, lae, log_sig)
                k.append(jnp.where(pos, 1.0 - lbh, 1.0) * jax.nn.sigmoid(-zf))
            log_f.append(lf)
        bc_all = _dot_split_rhs(tril_bf, jnp.concatenate(log_f, axis=1))
        yield
        bc = [bc_all[:, hd * DH:(hd + 1) * DH] for hd in range(HEADS)]
        o = [jnp.sum(q[hd] * k[hd], axis=-1, keepdims=True) * v[hd] for hd in range(HEADS)]
        pmat = None
        half = 1
        while half < L:
            blk = 2 * half
            second = (row128 & (blk - 1)) >= half
            qs, ks = [], []
            for hd in range(HEADS):
                zd = jnp.exp(-jnp.abs(bc[hd] - _mid_rows(bc[hd], half)))
                qs.append(_bf16(jnp.where(second, q[hd] * zd, 0.0)))
                ks.append(_bf16(jnp.where(second, 0.0, k[hd] * zd)))
            pl_ = _dot(jnp.concatenate(qs, axis=1), _block_rows(ks), _NT)
            if blk < L:
                pl_ = jnp.where((ci & -blk) == (cj & -blk), pl_, 0.0)
            pmat = pl_ if pmat is None else pmat + pl_
            yield
            half = blk
        v_bf = [_bf16(x) for x in v]
        o_intra = _dot(_bf16(pmat), _block_rows(v_bf))
        yield
        bc_last = [x[L - 1:L, :] for x in bc]
        hgrn[c] = dict(
            o=[o[hd] + o_intra[:, hd * DH:(hd + 1) * DH] for hd in range(HEADS)],
            qd=[_bf16(jnp.concatenate([q[hd] * jnp.exp(bc[hd]) for hd in (2 * p, 2 * p + 1)], axis=1))
                for p in range(HEADS // 2)],
            kd=[_bf16(k[hd] * jnp.exp(bc_last[hd] - bc[hd])) for hd in range(HEADS)],
            v=v_bf,
            e_last=[jnp.exp(x) for x in bc_last],
        )

    _run_round_robin([gdn_prep(c) for c in range(n_chunks)] + [hgrn_prep(c) for c in range(n_chunks)])

    def gdn_seq():
        s = [sa_ref[hd] for hd in range(HEADS)]
        for c in range(n_chunks):
            r0 = c * L
            g = gdn[c]
            s_bf = [_bf16(x) for x in s]
            ws_pair = [_dot(g["wq"][p], _block_rows(s_bf[2 * p:2 * p + 2])) for p in range(HEADS // 2)]
            ws = [ws_pair[hd // 2][:, (hd % 2) * DH:(hd % 2 + 1) * DH] for hd in range(HEADS)]
            yield
            v_new = [_bf16(g["u"][hd] - ws[hd][:L]) for hd in range(HEADS)]
            s = [s[hd] * g["eg_last"][hd] + _dot(g["kr"][hd], v_new[hd], _TN) for hd in range(HEADS)]
            sv = _dot(g["scores"], _block_rows(v_new))
            yield
            for hd in range(HEADS):
                o = ws[hd][L:] + sv[:, hd * DH:(hd + 1) * DH]
                z = proj_ref[r0:r0 + L, OFF_AZ + hd * DH:OFF_AZ + (hd + 1) * DH]
                y = o * lax.rsqrt(jnp.mean(o * o, axis=-1, keepdims=True) + EPS) * gna * _silu(z)
                ya_ref[r0:r0 + L, hd * DH:(hd + 1) * DH] = _bf16(y)
        for hd in range(HEADS):
            sa_ref[hd] = s[hd]

    def hgrn_seq():
        st = [sb_ref[hd] for hd in range(HEADS)]
        for c in range(n_chunks):
            r0 = c * L
            g = hgrn[c]
            st_bf = [_bf16(x) for x in st]
            o_pair = [_dot(g["qd"][p], _block_rows(st_bf[2 * p:2 * p + 2]), _NT) for p in range(HEADS // 2)]
            o_inter = [o_pair[hd // 2][:, (hd % 2) * DH:(hd % 2 + 1) * DH] for hd in range(HEADS)]
            st = [st[hd] * g["e_last"][hd] + _dot(g["v"][hd], g["kd"][hd], _TN) for hd in range(HEADS)]
            yield
            for hd in range(HEADS):
                o = g["o"][hd] + o_inter[hd]
                zg = proj_ref[r0:r0 + L, OFF_BG + hd * DH:OFF_BG + (hd + 1) * DH]
                y = o * lax.rsqrt(jnp.mean(o * o, axis=-1, keepdims=True) + EPS) * gnb * _silu(zg)
                yb_ref[r0:r0 + L, hd * DH:(hd + 1) * DH] = _bf16(y)
            yield
        for hd in range(HEADS):
            sb_ref[hd] = st[hd]

    _run_round_robin([gdn_seq(), hgrn_seq()])

    convbuf[0:SUBLANES, :] = convbuf[tt:tt + SUBLANES, :]

    ya = _dot(ya_ref[...], wba_ref[...])
    yb = _dot(yb_ref[...], wbb_ref[...])
    mixed = (jax.nn.sigmoid(proj_ref[:, OFF_GA:OFF_GA + d_model]) * ya
             + jax.nn.sigmoid(proj_ref[:, off_gb:off_gb + d_model]) * yb)
    out = h_ref[...] + _dot(_bf16(mixed), wout_ref[...])
    if final_norm:
        out = out * lax.rsqrt(jnp.mean(out * out, axis=-1, keepdims=True) + EPS) * fnw_ref[...]
    out_ref[...] = out

    if emit_state:
        sa_out_ref[...] = sa_ref[...]
        sb_out_ref[...] = sb_ref[...]
        conv_out_ref[...] = convbuf[0:SUBLANES, :]


def _layer(h, p, layer, depth, states, *, emit_state, final_norm):
    b, t, d = h.shape
    tt = min(t, MAX_TILE_ROWS)
    assert t % tt == 0 and tt % CHUNK == 0
    width_rest = p["w_in"].shape[1] - QKV_W
    f32 = jnp.float32

    def const(shape):
        return pl.BlockSpec(shape, lambda i, j: (0,) * len(shape))

    in_specs = [
        pl.BlockSpec((None, tt, d), lambda i, j: (i, j, 0)),
        const((1, d)),
        const(p["w_in"].shape),
        const((SUBLANES, QKV_W)),
        const((1, DH)), const((1, DH)), const((1, DH)), const((1, DH)),
        const((depth, WK)),
        const((WK, d)), const((WK, d)), const((d, d)),
        const((1, d)),
        const((HEADS, DH, DH)), const((HEADS, DH, DH)), const((SUBLANES, QKV_W)),
    ]
    out_shape = [jax.ShapeDtypeStruct((b, t, d), f32)]
    out_specs = [pl.BlockSpec((None, tt, d), lambda i, j: (i, j, 0))]
    if emit_state:
        assert b == 1 and t == tt
        out_shape += [jax.ShapeDtypeStruct((HEADS, DH, DH), f32), jax.ShapeDtypeStruct((HEADS, DH, DH), f32),
                      jax.ShapeDtypeStruct((SUBLANES, QKV_W), f32)]
        out_specs += [const((HEADS, DH, DH)), const((HEADS, DH, DH)), const((SUBLANES, QKV_W))]
    scratch = [
        pltpu.VMEM((tt, width_rest), f32),
        pltpu.VMEM((tt + SUBLANES, QKV_W), f32),
        pltpu.VMEM((HEADS, DH, DH), f32),
        pltpu.VMEM((HEADS, DH, DH), f32),
        pltpu.VMEM((tt, WK), MXU_DTYPE),
        pltpu.VMEM((tt, WK), MXU_DTYPE),
    ]
    kern = functools.partial(_layer_kernel, tile_rows=tt, d_model=d, depth=depth, layer=layer,
                             emit_state=emit_state, final_norm=final_norm)
    outs = pl.pallas_call(
        kern,
        grid=(b, t // tt),
        in_specs=in_specs,
        out_specs=out_specs,
        out_shape=out_shape,
        scratch_shapes=scratch,
        compiler_params=pltpu.CompilerParams(
            dimension_semantics=("arbitrary", "arbitrary"),
            vmem_limit_bytes=56 * 1024 * 1024),
        name=f"layer{layer}_{'meta' if emit_state else 'main'}",
    )(h, p["norm_w"], p["w_in"], p["conv_w"], p["negexp"], p["dtb"], p["gna"], p["gnb"], p["hlb"],
      p["wba"], p["wbb"], p["wout"], p["fnw"], *states)
    return outs


def _prep_layer(l, norm_w, w_in, conv_w, a_log, dt_bias, gnorm_a, gnorm_b, hlb, w_branch_a, w_branch_b,
                w_out, final_norm_w):
    d = w_in.shape[1]
    f32 = jnp.float32
    wl = w_in[l]
    o = [0]
    for s in (WK, WK, WK, HEADS, HEADS, WK, WK, WK, WK, WK, d, d):
        o.append(o[-1] + s)
    small = jnp.pad(wl[:, o[3]:o[5]], ((0, 0), (0, DH - 2 * HEADS)))
    w_re = jnp.concatenate([wl[:, o[0]:o[3]], wl[:, o[5]:o[12]], small], axis=1)
    lane_pad = (HEADS, DH - 2 * HEADS)
    return {
        "norm_w": norm_w[l][None].astype(f32),
        "w_in": w_re.astype(MXU_DTYPE),
        "conv_w": jnp.pad(conv_w[l].astype(f32), ((0, SUBLANES - CONV_K), (0, 0))),
        "negexp": jnp.pad(-jnp.exp(a_log[l].astype(f32)), lane_pad)[None],
        "dtb": jnp.pad(dt_bias[l].astype(f32), lane_pad)[None],
        "gna": gnorm_a[l][None].astype(f32),
        "gnb": gnorm_b[l][None].astype(f32),
        "hlb": hlb.astype(f32),
        "wba": w_branch_a[l].astype(MXU_DTYPE),
        "wbb": w_branch_b[l].astype(MXU_DTYPE),
        "wout": w_out[l].astype(MXU_DTYPE),
        "fnw": final_norm_w[None].astype(f32),
    }


def kernel(x, meta_tokens, norm_w, w_in, conv_w, a_log, dt_bias, gnorm_a, gnorm_b, hgrn_lower_bounds,
           w_branch_a, w_branch_b, w_out, final_norm_w):
    depth = w_in.shape[0]
    f32 = jnp.float32
    assert N_META <= CHUNK
    h = x
    hm = jnp.pad(meta_tokens.astype(x.dtype), ((CHUNK - N_META, 0), (0, 0)))[None]
    zero_state = (jnp.zeros((HEADS, DH, DH), f32), jnp.zeros((HEADS, DH, DH), f32),
                  jnp.zeros((SUBLANES, QKV_W), f32))
    for l in range(depth):
        p = _prep_layer(l, norm_w, w_in, conv_w, a_log, dt_bias, gnorm_a, gnorm_b, hgrn_lower_bounds,
                        w_branch_a, w_branch_b, w_out, final_norm_w)
        hm, s_a, s_b, conv_hist = _layer(hm, p, l, depth, zero_state, emit_state=True, final_norm=False)
        (h,) = _layer(h, p, l, depth, (s_a, s_b, conv_hist), emit_state=False, final_norm=(l == depth - 1))
    return h
```

```python
import functools

import jax
import jax.numpy as jnp
from jax import lax
from jax.experimental import pallas as pl
from jax.experimental.pallas import tpu as pltpu

N_META = 16
EPS = 1e-6
HEADS = 4
DH = 128
WK = HEADS * DH
CONV_K = 4
CHUNK = 64
CATW = HEADS * CHUNK
PAIR = 8
SUBLANES = 8
MAX_TILE_ROWS = 512

QKV_W = 3 * WK
OFF_AZ = 0
OFF_BQ = OFF_AZ + WK
OFF_BF = OFF_BQ + WK
OFF_BI = OFF_BF + WK
OFF_BG = OFF_BI + WK
OFF_GA = OFF_BG + WK


MXU_DTYPE = jnp.bfloat16


def _bf16(x):
    return x.astype(MXU_DTYPE)


def _dot(a, b, dims=(((1,), (0,)), ((), ()))):
    return lax.dot_general(a, b, dims, preferred_element_type=jnp.float32)


_NT = (((1,), (1,)), ((), ()))
_TN = (((0,), (0,)), ((), ()))


def _dot_split_rhs(a_bf16, b_f32):
    hi = _bf16(b_f32)
    mid = _bf16(b_f32 - hi.astype(jnp.float32))
    return _dot(a_bf16, hi) + _dot(a_bf16, mid)


def _log1pexp_neg_abs(x):
    return jnp.log(1.0 + jnp.exp(-jnp.abs(x)))


def _softplus(x):
    return jnp.maximum(x, 0.0) + _log1pexp_neg_abs(x)


def _mid_rows(x, half):
    n = x.shape[0]
    blk = 2 * half
    if blk >= 2 * SUBLANES:
        return jnp.concatenate(
            [jnp.broadcast_to(x[b0 + half - 1:b0 + half, :], (blk, x.shape[1])) for b0 in range(0, n, blk)], axis=0)
    x3 = x.reshape(n // SUBLANES, SUBLANES, x.shape[1])
    sub = lax.broadcasted_iota(jnp.int32, x3.shape, 1)
    out = None
    for b0 in range(0, SUBLANES, blk):
        piece = jnp.broadcast_to(x3[:, b0 + half - 1:b0 + half, :], x3.shape)
        out = piece if out is None else jnp.where(sub >= b0, piece, out)
    return out.reshape(x.shape)


def _silu(x):
    return x * jax.nn.sigmoid(x)


def _block_rows(blocks):
    n = len(blocks)
    rows = []
    for i, blk in enumerate(blocks):
        zero = jnp.zeros_like(blk)
        rows.append(jnp.concatenate([blk if j == i else zero for j in range(n)], axis=1))
    return jnp.concatenate(rows, axis=0)


def _run_round_robin(tasks):
    tasks = list(tasks)
    while tasks:
        for t in list(tasks):
            try:
                next(t)
            except StopIteration:
                tasks.remove(t)


def _layer_kernel(*refs, tile_rows, d_model, depth, layer, emit_state, final_norm):
    (h_ref, normw_ref, win_ref, convw_ref, negexp_ref, dtb_ref, gna_ref, gnb_ref, hlb_ref,
     wba_ref, wbb_ref, wout_ref, fnw_ref, sa0_ref, sb0_ref, conv0_ref) = refs[:16]
    if emit_state:
        out_ref, sa_out_ref, sb_out_ref, conv_out_ref = refs[16:20]
        scratch = refs[20:]
    else:
        out_ref = refs[16]
        scratch = refs[17:]
    proj_ref, convbuf, sa_ref, sb_ref, ya_ref, yb_ref = scratch

    f32 = jnp.float32
    tt = tile_rows
    width_rest = proj_ref.shape[1]
    off_gb = OFF_GA + d_model
    off_sm = off_gb + d_model
    L = CHUNK
    n_chunks = tt // L

    @pl.when(pl.program_id(1) == 0)
    def _():
        sa_ref[...] = sa0_ref[...]
        sb_ref[...] = sb0_ref[...]
        convbuf[0:SUBLANES, :] = conv0_ref[...]

    h = h_ref[...]
    xn = h * lax.rsqrt(jnp.mean(h * h, axis=-1, keepdims=True) + EPS) * normw_ref[...]
    xn = _bf16(xn)
    convbuf[SUBLANES:SUBLANES + tt, :] = _dot(xn, win_ref[:, 0:QKV_W])
    col = 0
    while col < width_rest:
        nxt = min(col + 1024, width_rest)
        proj_ref[:, col:nxt] = _dot(xn, win_ref[:, QKV_W + col:QKV_W + nxt])
        col = nxt

    hlb_rows = [hlb_ref[i:i + 1, :] for i in range(depth)]
    hmax = functools.reduce(jnp.maximum, hlb_rows)
    hexp = [jnp.exp(r - hmax) for r in hlb_rows]
    hsum = functools.reduce(lambda a, b: a + b, hexp)
    lb = jnp.zeros_like(hmax)
    for i in range(1, layer + 1):
        lb = lb + hexp[i] / hsum

    ci = lax.broadcasted_iota(jnp.int32, (L, CATW), 0)
    cj = lax.broadcasted_iota(jnp.int32, (L, CATW), 1) & (L - 1)
    causal_cat = ci >= cj
    strict_cat = ci > cj
    eye_cat = jnp.where(ci == cj, 1.0, 0.0).astype(f32)
    pair_cat = (ci & -PAIR) == (cj & -PAIR)
    bi = lax.broadcasted_iota(jnp.int32, (CATW, CATW), 0)
    bj = lax.broadcasted_iota(jnp.int32, (CATW, CATW), 1)
    bd_mask = (bi & -L) == (bj & -L)
    row = lax.broadcasted_iota(jnp.int32, (L, L), 0)
    colj = lax.broadcasted_iota(jnp.int32, (L, L), 1)
    tril_bf = _bf16(jnp.where(row >= colj, 1.0, 0.0))
    lower_left = {s: ((ci & -(2 * s)) == (cj & -(2 * s))) & ((ci & s) != 0) & ((cj & s) == 0)
                  for s in (PAIR, 2 * PAIR, 4 * PAIR)}
    same_block = {blk: (ci & -blk) == (cj & -blk) for blk in (2, 4, 8, 16, 32)}
    row128 = lax.broadcasted_iota(jnp.int32, (L, DH), 0)
    lane128 = lax.broadcasted_iota(jnp.int32, (L, DH), 1)
    convw = convw_ref[...]
    negexp = negexp_ref[...]
    dtb = dtb_ref[...]
    gna = gna_ref[...]
    gnb = gnb_ref[...]

    def block_diag(cat):
        return _bf16(jnp.where(bd_mask, jnp.concatenate([cat] * HEADS, axis=0), 0.0))

    def conv_silu(r0, c0):
        acc = convbuf[r0 + SUBLANES:r0 + SUBLANES + L, c0:c0 + DH] * convw[CONV_K - 1:CONV_K, c0:c0 + DH]
        for j in range(1, CONV_K):
            acc = acc + (convbuf[r0 + SUBLANES - j:r0 + SUBLANES - j + L, c0:c0 + DH]
                         * convw[CONV_K - 1 - j:CONV_K - j, c0:c0 + DH])
        return _silu(acc)

    gdn = [None] * n_chunks
    hgrn = [None] * n_chunks

    def gdn_prep(c):
        r0 = c * L
        small = proj_ref[r0:r0 + L, off_sm:off_sm + DH]
        beta_all = jax.nn.sigmoid(small)
        g_all = negexp * _softplus(small + dtb)
        beta_b = [jnp.broadcast_to(beta_all[:, hd:hd + 1], (L, DH)) for hd in range(HEADS)]
        g_b = [jnp.broadcast_to(g_all[:, HEADS + hd:HEADS + hd + 1], (L, DH)) for hd in range(HEADS)]
        g_cat = jnp.concatenate([jnp.where(lane128 < L, g_b[2 * t], g_b[2 * t + 1]) for t in range(HEADS // 2)],
                                axis=1)
        dec = _dot_split_rhs(tril_bf, jnp.concatenate([jnp.where(strict_cat, g_cat, 0.0)] + g_b, axis=1))
        q, k, v = [], [], []
        for hd in range(HEADS):
            qc = conv_silu(r0, hd * DH)
            kc = conv_silu(r0, WK + hd * DH)
            v.append(conv_silu(r0, 2 * WK + hd * DH))
            q.append(qc * lax.rsqrt(jnp.sum(qc * qc, axis=-1, keepdims=True) + EPS) * (DH ** -0.5))
            k.append(kc * lax.rsqrt(jnp.sum(kc * kc, axis=-1, keepdims=True) + EPS))
        yield
        decay = jnp.exp(dec[:, :CATW])
        gb = [dec[:, CATW + hd * DH:CATW + (hd + 1) * DH] for hd in range(HEADS)]
        kb = [k[hd] * beta_b[hd] for hd in range(HEADS)]
        k_bf = [_bf16(x) for x in k]
        lhs = jnp.concatenate([jnp.concatenate([_bf16(x) for x in kb], axis=1),
                               jnp.concatenate([_bf16(x) for x in q], axis=1)], axis=0)
        kkqk = _dot(lhs, _block_rows(k_bf), _NT)
        yield
        lcat = jnp.where(strict_cat, kkqk[:L] * decay, 0.0)
        scores = _bf16(jnp.where(causal_cat, kkqk[L:] * decay, 0.0))
        xb = jnp.where(pair_cat, -lcat, 0.0)
        x2 = _dot(_bf16(xb), block_diag(xb))
        yield
        bd2 = block_diag(x2)
        ipx = eye_cat + xb
        x4 = _dot(_bf16(x2), bd2)
        p1 = ipx + _dot(_bf16(ipx), bd2)
        yield
        inv = p1 + _dot(_bf16(p1), block_diag(x4))
        yield
        s = PAIR
        while s < L:
            e = _dot(_bf16(jnp.where(lower_left[s], lcat, 0.0)), block_diag(inv))
            yield
            inv = inv - _dot(_bf16(inv), block_diag(e))
            yield
            s *= 2
        eg = [jnp.exp(x) for x in gb]
        rhs = [_bf16(jnp.concatenate([v[hd] * beta_b[hd], kb[hd] * eg[hd]], axis=1)) for hd in range(HEADS)]
        sol = _dot(_bf16(inv), _block_rows(rhs))
        yield
        g_last = [x[L - 1:L, :] for x in gb]
        gdn[c] = dict(
            u=[sol[:, 2 * hd * DH:(2 * hd + 1) * DH] for hd in range(HEADS)],
            wq=[_bf16(jnp.concatenate(
                [jnp.concatenate([sol[:, (2 * hd + 1) * DH:(2 * hd + 2) * DH], q[hd] * eg[hd]], axis=0)
                 for hd in (2 * p, 2 * p + 1)], axis=1)) for p in range(HEADS // 2)],
            kr=[_bf16(jnp.concatenate([k[hd] * jnp.exp(g_last[hd] - gb[hd]) for hd in (2 * p, 2 * p + 1)], axis=0))
                for p in range(HEADS // 2)],
            eg_last=[jnp.exp(x) for x in g_last],
            scores=scores,
        )

    def hgrn_prep(c):
        r0 = c * L
        q, k, v, log_f = [], [], [], []
        for hd in range(HEADS):
            cs = slice(hd * DH, (hd + 1) * DH)
            bq = proj_ref[r0:r0 + L, OFF_BQ + hd * DH:OFF_BQ + (hd + 1) * DH]
            zf = proj_ref[r0:r0 + L, OFF_BF + hd * DH:OFF_BF + (hd + 1) * DH]
            v.append(proj_ref[r0:r0 + L, OFF_BI + hd * DH:OFF_BI + (hd + 1) * DH])
            q.append(_silu(bq) * (DH ** -0.5))
            log_sig = -_softplus(-zf)
            if layer == 0:
                lf = log_sig
                k.append(jax.nn.sigmoid(-zf))
            else:
                lbh = lb[:, cs]
                pos = lbh > 0.0
                la = jnp.log(jnp.where(pos, lbh, 1.0))
                lc = jnp.log1p(-lbh) + log_sig
                lae = jnp.maximum(la, lc) + _log1pexp_neg_abs(la - lc)
                lf = jnp.where(pos, lae, log_sig)
                k.append(jnp.where(pos, 1.0 - lbh, 1.0) * jax.nn.sigmoid(-zf))
            log_f.append(lf)
        bc_all = _dot_split_rhs(tril_bf, jnp.concatenate(log_f, axis=1))
        yield
        bc = [bc_all[:, hd * DH:(hd + 1) * DH] for hd in range(HEADS)]
        o = [jnp.sum(q[hd] * k[hd], axis=-1, keepdims=True) * v[hd] for hd in range(HEADS)]
        pmat = None
        half = 1
        while half < L:
            blk = 2 * half
            second = (row128 & (blk - 1)) >= half
            qs, ks = [], []
            for hd in range(HEADS):
                zd = jnp.exp(-jnp.abs(bc[hd] - _mid_rows(bc[hd], half)))
                qs.append(_bf16(jnp.where(second, q[hd] * zd, 0.0)))
                ks.append(_bf16(jnp.where(second, 0.0, k[hd] * zd)))
            pl_ = _dot(jnp.concatenate(qs, axis=1), _block_rows(ks), _NT)
            if blk < L:
                pl_ = jnp.where(same_block[blk], pl_, 0.0)
            pmat = pl_ if pmat is None else pmat + pl_
            yield
            half = blk
        v_bf = [_bf16(x) for x in v]
        o_intra = _dot(_bf16(pmat), _block_rows(v_bf))
        yield
        bc_last = [x[L - 1:L, :] for x in bc]
        hgrn[c] = dict(
            o=[o[hd] + o_intra[:, hd * DH:(hd + 1) * DH] for hd in range(HEADS)],
            qd=[_bf16(jnp.concatenate([q[hd] * jnp.exp(bc[hd]) for hd in (2 * p, 2 * p + 1)], axis=1))
                for p in range(HEADS // 2)],
            kd=[_bf16(k[hd] * jnp.exp(bc_last[hd] - bc[hd])) for hd in range(HEADS)],
            v2=[jnp.concatenate(v_bf[2 * p:2 * p + 2], axis=0) for p in range(HEADS // 2)],
            e_last=[jnp.exp(x) for x in bc_last],
        )

    _run_round_robin([gdn_prep(c) for c in range(n_chunks)] + [hgrn_prep(c) for c in range(n_chunks)])

    def gdn_seq():
        s = [sa_ref[hd] for hd in range(HEADS)]
        for c in range(n_chunks):
            r0 = c * L
            g = gdn[c]
            s_bf = [_bf16(x) for x in s]
            ws_pair = [_dot(g["wq"][p], _block_rows(s_bf[2 * p:2 * p + 2])) for p in range(HEADS // 2)]
            ws = [ws_pair[hd // 2][:, (hd % 2) * DH:(hd % 2 + 1) * DH] for hd in range(HEADS)]
            yield
            v_new = [_bf16(g["u"][hd] - ws[hd][:L]) for hd in range(HEADS)]
            kv = [_dot(g["kr"][p], _block_rows(v_new[2 * p:2 * p + 2]), _TN) for p in range(HEADS // 2)]
            s = [s[hd] * g["eg_last"][hd] + kv[hd // 2][:, (hd % 2) * DH:(hd % 2 + 1) * DH] for hd in range(HEADS)]
            sv = _dot(g["scores"], _block_rows(v_new))
            yield
            for hd in range(HEADS):
                o = ws[hd][L:] + sv[:, hd * DH:(hd + 1) * DH]
                z = proj_ref[r0:r0 + L, OFF_AZ + hd * DH:OFF_AZ + (hd + 1) * DH]
                y = o * lax.rsqrt(jnp.mean(o * o, axis=-1, keepdims=True) + EPS) * gna * _silu(z)
                ya_ref[r0:r0 + L, hd * DH:(hd + 1) * DH] = _bf16(y)
        for hd in range(HEADS):
            sa_ref[hd] = s[hd]

    def hgrn_seq():
        st = [sb_ref[hd] for hd in range(HEADS)]
        for c in range(n_chunks):
            r0 = c * L
            g = hgrn[c]
            st_bf = [_bf16(x) for x in st]
            o_pair = [_dot(g["qd"][p], _block_rows(st_bf[2 * p:2 * p + 2]), _NT) for p in range(HEADS // 2)]
            o_inter = [o_pair[hd // 2][:, (hd % 2) * DH:(hd % 2 + 1) * DH] for hd in range(HEADS)]
            vk = [_dot(g["v2"][p], _block_rows(g["kd"][2 * p:2 * p + 2]), _TN) for p in range(HEADS // 2)]
            st = [st[hd] * g["e_last"][hd] + vk[hd // 2][:, (hd % 2) * DH:(hd % 2 + 1) * DH] for hd in range(HEADS)]
            yield
            for hd in range(HEADS):
                o = g["o"][hd] + o_inter[hd]
                zg = proj_ref[r0:r0 + L, OFF_BG + hd * DH:OFF_BG + (hd + 1) * DH]
                y = o * lax.rsqrt(jnp.mean(o * o, axis=-1, keepdims=True) + EPS) * gnb * _silu(zg)
                yb_ref[r0:r0 + L, hd * DH:(hd + 1) * DH] = _bf16(y)
            yield
        for hd in range(HEADS):
            sb_ref[hd] = st[hd]

    _run_round_robin([gdn_seq(), hgrn_seq()])

    convbuf[0:SUBLANES, :] = convbuf[tt:tt + SUBLANES, :]

    ya = _dot(ya_ref[...], wba_ref[...])
    yb = _dot(yb_ref[...], wbb_ref[...])
    mixed = (jax.nn.sigmoid(proj_ref[:, OFF_GA:OFF_GA + d_model]) * ya
             + jax.nn.sigmoid(proj_ref[:, off_gb:off_gb + d_model]) * yb)
    out = h_ref[...] + _dot(_bf16(mixed), wout_ref[...])
    if final_norm:
        out = out * lax.rsqrt(jnp.mean(out * out, axis=-1, keepdims=True) + EPS) * fnw_ref[...]
    out_ref[...] = out

    if emit_state:
        sa_out_ref[...] = sa_ref[...]
        sb_out_ref[...] = sb_ref[...]
        conv_out_ref[...] = convbuf[0:SUBLANES, :]


def _layer(h, p, layer, depth, states, *, emit_state, final_norm):
    b, t, d = h.shape
    tt = min(t, MAX_TILE_ROWS)
    assert t % tt == 0 and tt % CHUNK == 0
    width_rest = p["w_in"].shape[1] - QKV_W
    f32 = jnp.float32

    def const(shape):
        return pl.BlockSpec(shape, lambda i, j: (0,) * len(shape), pipeline_mode=pl.Buffered(1))

    in_specs = [
        pl.BlockSpec((None, tt, d), lambda i, j: (i, j, 0)),
        const((1, d)),
        const(p["w_in"].shape),
        const((SUBLANES, QKV_W)),
        const((1, DH)), const((1, DH)), const((1, DH)), const((1, DH)),
        const((depth, WK)),
        const((WK, d)), const((WK, d)), const((d, d)),
        const((1, d)),
        const((HEADS, DH, DH)), const((HEADS, DH, DH)), const((SUBLANES, QKV_W)),
    ]
    out_shape = [jax.ShapeDtypeStruct((b, t, d), f32)]
    out_specs = [pl.BlockSpec((None, tt, d), lambda i, j: (i, j, 0))]
    if emit_state:
        assert b == 1 and t == tt
        out_shape += [jax.ShapeDtypeStruct((HEADS, DH, DH), f32), jax.ShapeDtypeStruct((HEADS, DH, DH), f32),
                      jax.ShapeDtypeStruct((SUBLANES, QKV_W), f32)]
        out_specs += [const((HEADS, DH, DH)), const((HEADS, DH, DH)), const((SUBLANES, QKV_W))]
    scratch = [
        pltpu.VMEM((tt, width_rest), f32),
        pltpu.VMEM((tt + SUBLANES, QKV_W), f32),
        pltpu.VMEM((HEADS, DH, DH), f32),
        pltpu.VMEM((HEADS, DH, DH), f32),
        pltpu.VMEM((tt, WK), MXU_DTYPE),
        pltpu.VMEM((tt, WK), MXU_DTYPE),
    ]
    kern = functools.partial(_layer_kernel, tile_rows=tt, d_model=d, depth=depth, layer=layer,
                             emit_state=emit_state, final_norm=final_norm)
    outs = pl.pallas_call(
        kern,
        grid=(b, t // tt),
        in_specs=in_specs,
        out_specs=out_specs,
        out_shape=out_shape,
        scratch_shapes=scratch,
        compiler_params=pltpu.CompilerParams(
            dimension_semantics=("arbitrary", "arbitrary"),
            vmem_limit_bytes=56 * 1024 * 1024),
        name=f"layer{layer}_{'meta' if emit_state else 'main'}",
    )(h, p["norm_w"], p["w_in"], p["conv_w"], p["negexp"], p["dtb"], p["gna"], p["gnb"], p["hlb"],
      p["wba"], p["wbb"], p["wout"], p["fnw"], *states)
    return outs


def _prep_layer(l, norm_w, w_in, conv_w, a_log, dt_bias, gnorm_a, gnorm_b, hlb, w_branch_a, w_branch_b,
                w_out, final_norm_w):
    d = w_in.shape[1]
    f32 = jnp.float32
    wl = w_in[l]
    o = [0]
    for s in (WK, WK, WK, HEADS, HEADS, WK, WK, WK, WK, WK, d, d):
        o.append(o[-1] + s)
    small = jnp.pad(wl[:, o[3]:o[5]], ((0, 0), (0, DH - 2 * HEADS)))
    w_re = jnp.concatenate([wl[:, o[0]:o[3]], wl[:, o[5]:o[12]], small], axis=1)
    lane_pad = (HEADS, DH - 2 * HEADS)
    return {
        "norm_w": norm_w[l][None].astype(f32),
        "w_in": w_re.astype(MXU_DTYPE),
        "conv_w": jnp.pad(conv_w[l].astype(f32), ((0, SUBLANES - CONV_K), (0, 0))),
        "negexp": jnp.pad(-jnp.exp(a_log[l].astype(f32)), lane_pad)[None],
        "dtb": jnp.pad(dt_bias[l].astype(f32), lane_pad)[None],
        "gna": gnorm_a[l][None].astype(f32),
        "gnb": gnorm_b[l][None].astype(f32),
        "hlb": hlb.astype(f32),
        "wba": w_branch_a[l].astype(MXU_DTYPE),
        "wbb": w_branch_b[l].astype(MXU_DTYPE),
        "wout": w_out[l].astype(MXU_DTYPE),
        "fnw": final_norm_w[None].astype(f32),
    }


def kernel(x, meta_tokens, norm_w, w_in, conv_w, a_log, dt_bias, gnorm_a, gnorm_b, hgrn_lower_bounds,
           w_branch_a, w_branch_b, w_out, final_norm_w):
    depth = w_in.shape[0]
    f32 = jnp.float32
    assert N_META <= CHUNK
    h = x
    hm = jnp.pad(meta_tokens.astype(x.dtype), ((CHUNK - N_META, 0), (0, 0)))[None]
    zero_state = (jnp.zeros((HEADS, DH, DH), f32), jnp.zeros((HEADS, DH, DH), f32),
                  jnp.zeros((SUBLANES, QKV_W), f32))
    for l in range(depth):
        p = _prep_layer(l, norm_w, w_in, conv_w, a_log, dt_bias, gnorm_a, gnorm_b, hgrn_lower_bounds,
                        w_branch_a, w_branch_b, w_out, final_norm_w)
        hm, s_a, s_b, conv_hist = _layer(hm, p, l, depth, zero_state, emit_state=True, final_norm=False)
        (h,) = _layer(h, p, l, depth, (s_a, s_b, conv_hist), emit_state=False, final_norm=(l == depth - 1))
    return h
```

```python
import functools

import jax
import jax.numpy as jnp
from jax import lax
from jax.experimental import pallas as pl
from jax.experimental.pallas import tpu as pltpu

N_META = 16
EPS = 1e-6
HEADS = 4
DH = 128
WK = HEADS * DH
CONV_K = 4
CHUNK = 64
CATW = HEADS * CHUNK
PAIR = 8
SUBLANES = 8
MAX_TILE_ROWS = 256
MAX_TILE_SEQS = 2

QKV_W = 3 * WK
OFF_AZ = 0
OFF_BQ = OFF_AZ + WK
OFF_BF = OFF_BQ + WK
OFF_BI = OFF_BF + WK
OFF_BG = OFF_BI + WK
OFF_GA = OFF_BG + WK


MXU_DTYPE = jnp.bfloat16
LOG2E = 1.4426950408889634


def _bf16(x):
    return x.astype(MXU_DTYPE)


def _dot(a, b, dims=(((1,), (0,)), ((), ()))):
    return lax.dot_general(a, b, dims, preferred_element_type=jnp.float32)


_NT = (((1,), (1,)), ((), ()))
_TN = (((0,), (0,)), ((), ()))


def _dot_split_rhs(a_bf16, b_f32):
    hi = _bf16(b_f32)
    mid = _bf16(b_f32 - hi.astype(jnp.float32))
    return _dot(a_bf16, hi) + _dot(a_bf16, mid)


def _log1pexp_neg_abs(x):
    return jnp.log(1.0 + jnp.exp(-jnp.abs(x)))


def _softplus(x):
    return jnp.maximum(x, 0.0) + _log1pexp_neg_abs(x)


def _mid_rows(x, half):
    n = x.shape[0]
    blk = 2 * half
    if blk >= 2 * SUBLANES:
        return jnp.concatenate(
            [jnp.broadcast_to(x[b0 + half - 1:b0 + half, :], (blk, x.shape[1])) for b0 in range(0, n, blk)], axis=0)
    x3 = x.reshape(n // SUBLANES, SUBLANES, x.shape[1])
    sub = lax.broadcasted_iota(jnp.int32, x3.shape, 1)
    out = None
    for b0 in range(0, SUBLANES, blk):
        piece = jnp.broadcast_to(x3[:, b0 + half - 1:b0 + half, :], x3.shape)
        out = piece if out is None else jnp.where(sub >= b0, piece, out)
    return out.reshape(x.shape)


def _silu(x):
    return x * jax.nn.sigmoid(x)


def _block_rows(blocks):
    n = len(blocks)
    rows = []
    for i, blk in enumerate(blocks):
        zero = jnp.zeros_like(blk)
        rows.append(jnp.concatenate([blk if j == i else zero for j in range(n)], axis=1))
    return jnp.concatenate(rows, axis=0)


def _run_round_robin(tasks):
    tasks = list(tasks)
    while tasks:
        for t in list(tasks):
            try:
                next(t)
            except StopIteration:
                tasks.remove(t)


def _layer_kernel(*refs, tile_rows, tile_seqs, d_model, depth, layer, emit_state, final_norm):
    (h_ref, normw_ref, win_ref, convw_ref, negexp_ref, dtb_ref, gna_ref, gnb_ref, hlb_ref,
     wba_ref, wbb_ref, wout_ref, fnw_ref, sa0_ref, sb0_ref, conv0_ref) = refs[:16]
    if emit_state:
        out_ref, sa_out_ref, sb_out_ref, conv_out_ref = refs[16:20]
        scratch = refs[20:]
    else:
        out_ref = refs[16]
        scratch = refs[17:]
    proj_ref, convbuf, sa_ref, sb_ref, ya_ref, yb_ref = scratch

    f32 = jnp.float32
    tt = tile_rows
    width_rest = proj_ref.shape[1]
    off_gb = OFF_GA + d_model
    off_sm = off_gb + d_model
    L = CHUNK
    n_chunks = tt // L
    ns = tile_seqs

    @pl.when(pl.program_id(1) == 0)
    def _():
        for sq in range(ns):
            sa_ref[sq] = sa0_ref[...]
            sb_ref[sq] = sb0_ref[...]
            convbuf[sq, 0:SUBLANES, :] = conv0_ref[...]

    h = h_ref[...].reshape(ns * tt, d_model)
    xn = h * lax.rsqrt(jnp.mean(h * h, axis=-1, keepdims=True) + EPS) * normw_ref[...]
    xn = _bf16(xn)
    qkv_raw = _dot(xn, win_ref[:, 0:QKV_W])
    for sq in range(ns):
        convbuf[sq, SUBLANES:SUBLANES + tt, :] = qkv_raw[sq * tt:(sq + 1) * tt]
    col = 0
    while col < width_rest:
        nxt = min(col + 1024, width_rest)
        proj_ref[:, col:nxt] = _dot(xn, win_ref[:, QKV_W + col:QKV_W + nxt])
        col = nxt

    hlb_rows = [hlb_ref[i:i + 1, :] for i in range(depth)]
    hmax = functools.reduce(jnp.maximum, hlb_rows)
    hexp = [jnp.exp(r - hmax) for r in hlb_rows]
    hsum = functools.reduce(lambda a, b: a + b, hexp)
    lb = jnp.zeros_like(hmax)
    for i in range(1, layer + 1):
        lb = lb + hexp[i] / hsum

    ci = lax.broadcasted_iota(jnp.int32, (L, CATW), 0)
    cj = lax.broadcasted_iota(jnp.int32, (L, CATW), 1) & (L - 1)
    causal_cat = ci >= cj
    strict_cat = ci > cj
    eye_cat = jnp.where(ci == cj, 1.0, 0.0).astype(f32)
    pair_cat = (ci & -PAIR) == (cj & -PAIR)
    bi = lax.broadcasted_iota(jnp.int32, (CATW, CATW), 0)
    bj = lax.broadcasted_iota(jnp.int32, (CATW, CATW), 1)
    bd_mask = (bi & -L) == (bj & -L)
    row = lax.broadcasted_iota(jnp.int32, (L, L), 0)
    colj = lax.broadcasted_iota(jnp.int32, (L, L), 1)
    tril_bf = _bf16(jnp.where(row >= colj, 1.0, 0.0))
    lower_left = {s: ((ci & -(2 * s)) == (cj & -(2 * s))) & ((ci & s) != 0) & ((cj & s) == 0)
                  for s in (PAIR, 2 * PAIR, 4 * PAIR)}
    same_block = {blk: (ci & -blk) == (cj & -blk) for blk in (2, 4, 8, 16, 32)}
    row128 = lax.broadcasted_iota(jnp.int32, (L, DH), 0)
    lane128 = lax.broadcasted_iota(jnp.int32, (L, DH), 1)
    convw = convw_ref[...]
    negexp = negexp_ref[...]
    dtb = dtb_ref[...]
    gna = gna_ref[...]
    gnb = gnb_ref[...]

    def block_diag(cat):
        return _bf16(jnp.where(bd_mask, jnp.concatenate([cat] * HEADS, axis=0), 0.0))

    def conv_silu(sq, r0, c0):
        acc = convbuf[sq, r0 + SUBLANES:r0 + SUBLANES + L, c0:c0 + DH] * convw[CONV_K - 1:CONV_K, c0:c0 + DH]
        for j in range(1, CONV_K):
            acc = acc + (convbuf[sq, r0 + SUBLANES - j:r0 + SUBLANES - j + L, c0:c0 + DH]
                         * convw[CONV_K - 1 - j:CONV_K - j, c0:c0 + DH])
        return _silu(acc)

    gdn = [[None] * n_chunks for _ in range(ns)]
    hgrn = [[None] * n_chunks for _ in range(ns)]

    def gdn_prep(sq, c):
        r0 = sq * tt + c * L
        small = proj_ref[r0:r0 + L, off_sm:off_sm + DH]
        beta_all = jax.nn.sigmoid(small)
        g_all = (negexp * LOG2E) * _softplus(small + dtb)
        beta_b = [jnp.broadcast_to(beta_all[:, hd:hd + 1], (L, DH)) for hd in range(HEADS)]
        g_b = [jnp.broadcast_to(g_all[:, HEADS + hd:HEADS + hd + 1], (L, DH)) for hd in range(HEADS)]
        g_cat = jnp.concatenate([jnp.where(lane128 < L, g_b[2 * t], g_b[2 * t + 1]) for t in range(HEADS // 2)],
                                axis=1)
        dec = _dot_split_rhs(tril_bf, jnp.concatenate([jnp.where(strict_cat, g_cat, 0.0)] + g_b, axis=1))
        q, k, v = [], [], []
        for hd in range(HEADS):
            qc = conv_silu(sq, c * L, hd * DH)
            kc = conv_silu(sq, c * L, WK + hd * DH)
            v.append(conv_silu(sq, c * L, 2 * WK + hd * DH))
            q.append(qc * lax.rsqrt(jnp.sum(qc * qc, axis=-1, keepdims=True) + EPS) * (DH ** -0.5))
            k.append(kc * lax.rsqrt(jnp.sum(kc * kc, axis=-1, keepdims=True) + EPS))
        yield
        decay = jnp.exp2(dec[:, :CATW])
        gb = [dec[:, CATW + hd * DH:CATW + (hd + 1) * DH] for hd in range(HEADS)]
        kb = [k[hd] * beta_b[hd] for hd in range(HEADS)]
        k_bf = [_bf16(x) for x in k]
        lhs = jnp.concatenate([jnp.concatenate([_bf16(x) for x in kb], axis=1),
                               jnp.concatenate([_bf16(x) for x in q], axis=1)], axis=0)
        kkqk = _dot(lhs, _block_rows(k_bf), _NT)
        yield
        lcat = jnp.where(strict_cat, kkqk[:L] * decay, 0.0)
        scores = _bf16(jnp.where(causal_cat, kkqk[L:] * decay, 0.0))
        xb = jnp.where(pair_cat, -lcat, 0.0)
        x2 = _dot(_bf16(xb), block_diag(xb))
        yield
        bd2 = block_diag(x2)
        ipx = eye_cat + xb
        x4 = _dot(_bf16(x2), bd2)
        p1 = ipx + _dot(_bf16(ipx), bd2)
        yield
        inv = p1 + _dot(_bf16(p1), block_diag(x4))
        yield
        s = PAIR
        while s < L:
            e = _dot(_bf16(jnp.where(lower_left[s], lcat, 0.0)), block_diag(inv))
            yield
            inv = inv - _dot(_bf16(inv), block_diag(e))
            yield
            s *= 2
        eg = [jnp.exp2(x) for x in gb]
        rhs = [_bf16(jnp.concatenate([v[hd] * beta_b[hd], kb[hd] * eg[hd]], axis=1)) for hd in range(HEADS)]
        sol = _dot(_bf16(inv), _block_rows(rhs))
        yield
        g_last = [x[L - 1:L, :] for x in gb]
        gdn[sq][c] = dict(
            u=[sol[:, 2 * hd * DH:(2 * hd + 1) * DH] for hd in range(HEADS)],
            wq=[_bf16(jnp.concatenate(
                [jnp.concatenate([sol[:, (2 * hd + 1) * DH:(2 * hd + 2) * DH], q[hd] * eg[hd]], axis=0)
                 for hd in (2 * p, 2 * p + 1)], axis=1)) for p in range(HEADS // 2)],
            kr=[_bf16(jnp.concatenate([k[hd] * jnp.exp2(g_last[hd] - gb[hd]) for hd in (2 * p, 2 * p + 1)], axis=0))
                for p in range(HEADS // 2)],
            eg_last=[jnp.exp2(x) for x in g_last],
            scores=scores,
        )

    def hgrn_prep(sq, c):
        r0 = sq * tt + c * L
        q, k, v, log_f = [], [], [], []
        for hd in range(HEADS):
            cs = slice(hd * DH, (hd + 1) * DH)
            bq = proj_ref[r0:r0 + L, OFF_BQ + hd * DH:OFF_BQ + (hd + 1) * DH]
            zf = proj_ref[r0:r0 + L, OFF_BF + hd * DH:OFF_BF + (hd + 1) * DH]
            v.append(proj_ref[r0:r0 + L, OFF_BI + hd * DH:OFF_BI + (hd + 1) * DH])
            q.append(_silu(bq) * (DH ** -0.5))
            log_sig = -_softplus(-zf)
            if layer == 0:
                lf = log_sig
                k.append(jax.nn.sigmoid(-zf))
            else:
                lbh = lb[:, cs]
                pos = lbh > 0.0
                la = jnp.log(jnp.where(pos, lbh, 1.0))
                lc = jnp.log1p(-lbh) + log_sig
                lae = jnp.maximum(la, lc) + _log1pexp_neg_abs(la - lc)
                lf = jnp.where(pos, lae, log_sig)
                k.append(jnp.where(pos, 1.0 - lbh, 1.0) * jax.nn.sigmoid(-zf))
            log_f.append(lf * LOG2E)
        bc_all = _dot_split_rhs(tril_bf, jnp.concatenate(log_f, axis=1))
        yield
        bc = [bc_all[:, hd * DH:(hd + 1) * DH] for hd in range(HEADS)]
        o = [jnp.sum(q[hd] * k[hd], axis=-1, keepdims=True) * v[hd] for hd in range(HEADS)]
        pmat = None
        half = 1
        while half < L:
            blk = 2 * half
            second = (row128 & (blk - 1)) >= half
            qs, ks = [], []
            for hd in range(HEADS):
                zd = jnp.exp2(-jnp.abs(bc[hd] - _mid_rows(bc[hd], half)))
                qs.append(_bf16(jnp.where(second, q[hd] * zd, 0.0)))
                ks.append(_bf16(jnp.where(second, 0.0, k[hd] * zd)))
            pl_ = _dot(jnp.concatenate(qs, axis=1), _block_rows(ks), _NT)
            if blk < L:
                pl_ = jnp.where(same_block[blk], pl_, 0.0)
            pmat = pl_ if pmat is None else pmat + pl_
            yield
            half = blk
        v_bf = [_bf16(x) for x in v]
        o_intra = _dot(_bf16(pmat), _block_rows(v_bf))
        yield
        bc_last = [x[L - 1:L, :] for x in bc]
        hgrn[sq][c] = dict(
            o=[o[hd] + o_intra[:, hd * DH:(hd + 1) * DH] for hd in range(HEADS)],
            qd=[_bf16(jnp.concatenate([q[hd] * jnp.exp2(bc[hd]) for hd in (2 * p, 2 * p + 1)], axis=1))
                for p in range(HEADS // 2)],
            kd=[_bf16(k[hd] * jnp.exp2(bc_last[hd] - bc[hd])) for hd in range(HEADS)],
            v2=[jnp.concatenate(v_bf[2 * p:2 * p + 2], axis=0) for p in range(HEADS // 2)],
            e_last=[jnp.exp2(x) for x in bc_last],
        )

    _run_round_robin([f(sq, c) for f in (gdn_prep, hgrn_prep) for sq in range(ns) for c in range(n_chunks)])

    def gdn_seq(sq):
        s = [sa_ref[sq, hd] for hd in range(HEADS)]
        for c in range(n_chunks):
            r0 = sq * tt + c * L
            g = gdn[sq][c]
            s_bf = [_bf16(x) for x in s]
            ws_pair = [_dot(g["wq"][p], _block_rows(s_bf[2 * p:2 * p + 2])) for p in range(HEADS // 2)]
            ws = [ws_pair[hd // 2][:, (hd % 2) * DH:(hd % 2 + 1) * DH] for hd in range(HEADS)]
            yield
            v_new = [_bf16(g["u"][hd] - ws[hd][:L]) for hd in range(HEADS)]
            kv = [_dot(g["kr"][p], _block_rows(v_new[2 * p:2 * p + 2]), _TN) for p in range(HEADS // 2)]
            s = [s[hd] * g["eg_last"][hd] + kv[hd // 2][:, (hd % 2) * DH:(hd % 2 + 1) * DH] for hd in range(HEADS)]
            sv = _dot(g["scores"], _block_rows(v_new))
            yield
            for hd in range(HEADS):
                o = ws[hd][L:] + sv[:, hd * DH:(hd + 1) * DH]
                z = proj_ref[r0:r0 + L, OFF_AZ + hd * DH:OFF_AZ + (hd + 1) * DH]
                y = o * lax.rsqrt(jnp.mean(o * o, axis=-1, keepdims=True) + EPS) * gna * _silu(z)
                ya_ref[r0:r0 + L, hd * DH:(hd + 1) * DH] = _bf16(y)
        for hd in range(HEADS):
            sa_ref[sq, hd] = s[hd]

    def hgrn_seq(sq):
        st = [sb_ref[sq, hd] for hd in range(HEADS)]
        for c in range(n_chunks):
            r0 = sq * tt + c * L
            g = hgrn[sq][c]
            st_bf = [_bf16(x) for x in st]
            o_pair = [_dot(g["qd"][p], _block_rows(st_bf[2 * p:2 * p + 2]), _NT) for p in range(HEADS // 2)]
            o_inter = [o_pair[hd // 2][:, (hd % 2) * DH:(hd % 2 + 1) * DH] for hd in range(HEADS)]
            vk = [_dot(g["v2"][p], _block_rows(g["kd"][2 * p:2 * p + 2]), _TN) for p in range(HEADS // 2)]
            st = [st[hd] * g["e_last"][hd] + vk[hd // 2][:, (hd % 2) * DH:(hd % 2 + 1) * DH] for hd in range(HEADS)]
            yield
            for hd in range(HEADS):
                o = g["o"][hd] + o_inter[hd]
                zg = proj_ref[r0:r0 + L, OFF_BG + hd * DH:OFF_BG + (hd + 1) * DH]
                y = o * lax.rsqrt(jnp.mean(o * o, axis=-1, keepdims=True) + EPS) * gnb * _silu(zg)
                yb_ref[r0:r0 + L, hd * DH:(hd + 1) * DH] = _bf16(y)
            yield
        for hd in range(HEADS):
            sb_ref[sq, hd] = st[hd]

    _run_round_robin([f(sq) for f in (gdn_seq, hgrn_seq) for sq in range(ns)])

    for sq in range(ns):
        convbuf[sq, 0:SUBLANES, :] = convbuf[sq, tt:tt + SUBLANES, :]

    ya = _dot(ya_ref[...], wba_ref[...])
    yb = _dot(yb_ref[...], wbb_ref[...])
    mixed = (jax.nn.sigmoid(proj_ref[:, OFF_GA:OFF_GA + d_model]) * ya
             + jax.nn.sigmoid(proj_ref[:, off_gb:off_gb + d_model]) * yb)
    out = h_ref[...].reshape(ns * tt, d_model) + _dot(_bf16(mixed), wout_ref[...])
    if final_norm:
        out = out * lax.rsqrt(jnp.mean(out * out, axis=-1, keepdims=True) + EPS) * fnw_ref[...]
    out_ref[...] = out.reshape(ns, tt, d_model)

    if emit_state:
        sa_out_ref[...] = sa_ref[0]
        sb_out_ref[...] = sb_ref[0]
        conv_out_ref[...] = convbuf[0, 0:SUBLANES, :]


def _layer(h, p, layer, depth, states, *, emit_state, final_norm):
    b, t, d = h.shape
    tt = min(t, MAX_TILE_ROWS)
    ns = min(b, MAX_TILE_SEQS)
    assert t % tt == 0 and tt % CHUNK == 0 and b % ns == 0
    width_rest = p["w_in"].shape[1] - QKV_W
    f32 = jnp.float32

    def const(shape):
        return pl.BlockSpec(shape, lambda i, j: (0,) * len(shape), pipeline_mode=pl.Buffered(1))

    in_specs = [
        pl.BlockSpec((ns, tt, d), lambda i, j: (i, j, 0)),
        const((1, d)),
        const(p["w_in"].shape),
        const((SUBLANES, QKV_W)),
        const((1, DH)), const((1, DH)), const((1, DH)), const((1, DH)),
        const((depth, WK)),
        const((WK, d)), const((WK, d)), const((d, d)),
        const((1, d)),
        const((HEADS, DH, DH)), const((HEADS, DH, DH)), const((SUBLANES, QKV_W)),
    ]
    out_shape = [jax.ShapeDtypeStruct((b, t, d), f32)]
    out_specs = [pl.BlockSpec((ns, tt, d), lambda i, j: (i, j, 0))]
    if emit_state:
        assert b == 1 and t == tt
        out_shape += [jax.ShapeDtypeStruct((HEADS, DH, DH), f32), jax.ShapeDtypeStruct((HEADS, DH, DH), f32),
                      jax.ShapeDtypeStruct((SUBLANES, QKV_W), f32)]
        out_specs += [const((HEADS, DH, DH)), const((HEADS, DH, DH)), const((SUBLANES, QKV_W))]
    scratch = [
        pltpu.VMEM((ns * tt, width_rest), f32),
        pltpu.VMEM((ns, tt + SUBLANES, QKV_W), f32),
        pltpu.VMEM((ns, HEADS, DH, DH), f32),
        pltpu.VMEM((ns, HEADS, DH, DH), f32),
        pltpu.VMEM((ns * tt, WK), MXU_DTYPE),
        pltpu.VMEM((ns * tt, WK), MXU_DTYPE),
    ]
    kern = functools.partial(_layer_kernel, tile_rows=tt, tile_seqs=ns, d_model=d, depth=depth, layer=layer,
                             emit_state=emit_state, final_norm=final_norm)
    outs = pl.pallas_call(
        kern,
        grid=(b // ns, t // tt),
        in_specs=in_specs,
        out_specs=out_specs,
        out_shape=out_shape,
        scratch_shapes=scratch,
        compiler_params=pltpu.CompilerParams(
            dimension_semantics=("arbitrary", "arbitrary"),
            vmem_limit_bytes=56 * 1024 * 1024),
        name=f"layer{layer}_{'meta' if emit_state else 'main'}",
    )(h, p["norm_w"], p["w_in"], p["conv_w"], p["negexp"], p["dtb"], p["gna"], p["gnb"], p["hlb"],
      p["wba"], p["wbb"], p["wout"], p["fnw"], *states)
    return outs


def _prep_layer(l, norm_w, w_in, conv_w, a_log, dt_bias, gnorm_a, gnorm_b, hlb, w_branch_a, w_branch_b,
                w_out, final_norm_w):
    d = w_in.shape[1]
    f32 = jnp.float32
    wl = w_in[l]
    o = [0]
    for s in (WK, WK, WK, HEADS, HEADS, WK, WK, WK, WK, WK, d, d):
        o.append(o[-1] + s)
    small = jnp.pad(wl[:, o[3]:o[5]], ((0, 0), (0, DH - 2 * HEADS)))
    w_re = jnp.concatenate([wl[:, o[0]:o[3]], wl[:, o[5]:o[12]], small], axis=1)
    lane_pad = (HEADS, DH - 2 * HEADS)
    return {
        "norm_w": norm_w[l][None].astype(f32),
        "w_in": w_re.astype(MXU_DTYPE),
        "conv_w": jnp.pad(conv_w[l].astype(f32), ((0, SUBLANES - CONV_K), (0, 0))),
        "negexp": jnp.pad(-jnp.exp(a_log[l].astype(f32)), lane_pad)[None],
        "dtb": jnp.pad(dt_bias[l].astype(f32), lane_pad)[None],
        "gna": gnorm_a[l][None].astype(f32),
        "gnb": gnorm_b[l][None].astype(f32),
        "hlb": hlb.astype(f32),
        "wba": w_branch_a[l].astype(MXU_DTYPE),
        "wbb": w_branch_b[l].astype(MXU_DTYPE),
        "wout": w_out[l].astype(MXU_DTYPE),
        "fnw": final_norm_w[None].astype(f32),
    }


def kernel(x, meta_tokens, norm_w, w_in, conv_w, a_log, dt_bias, gnorm_a, gnorm_b, hgrn_lower_bounds,
           w_branch_a, w_branch_b, w_out, final_norm_w):
    depth = w_in.shape[0]
    f32 = jnp.float32
    assert N_META <= CHUNK
    h = x
    hm = jnp.pad(meta_tokens.astype(x.dtype), ((CHUNK - N_META, 0), (0, 0)))[None]
    zero_state = (jnp.zeros((HEADS, DH, DH), f32), jnp.zeros((HEADS, DH, DH), f32),
                  jnp.zeros((SUBLANES, QKV_W), f32))
    for l in range(depth):
        p = _prep_layer(l, norm_w, w_in, conv_w, a_log, dt_bias, gnorm_a, gnorm_b, hgrn_lower_bounds,
                        w_branch_a, w_branch_b, w_out, final_norm_w)
        hm, s_a, s_b, conv_hist = _layer(hm, p, l, depth, zero_state, emit_state=True, final_norm=False)
        (h,) = _layer(h, p, l, depth, (s_a, s_b, conv_hist), emit_state=False, final_norm=(l == depth - 1))
    return h
```

```python
import functools

import jax
import jax.numpy as jnp
from jax import lax
from jax.experimental import pallas as pl
from jax.experimental.pallas import tpu as pltpu

N_META = 16
EPS = 1e-6
HEADS = 4
DH = 128
WK = HEADS * DH
CONV_K = 4
CHUNK = 64
CATW = HEADS * CHUNK
PAIR = 8
SUBLANES = 8
MAX_TILE_ROWS = 64
MAX_TILE_SEQS = 8

QKV_W = 3 * WK
OFF_AZ = 0
OFF_BQ = OFF_AZ + WK
OFF_BF = OFF_BQ + WK
OFF_BI = OFF_BF + WK
OFF_BG = OFF_BI + WK
OFF_GA = OFF_BG + WK


MXU_DTYPE = jnp.bfloat16
LOG2E = 1.4426950408889634


def _bf16(x):
    return x.astype(MXU_DTYPE)


def _dot(a, b, dims=(((1,), (0,)), ((), ()))):
    return lax.dot_general(a, b, dims, preferred_element_type=jnp.float32)


_NT = (((1,), (1,)), ((), ()))
_TN = (((0,), (0,)), ((), ()))


def _dot_split_rhs(a_bf16, b_f32):
    hi = _bf16(b_f32)
    mid = _bf16(b_f32 - hi.astype(jnp.float32))
    return _dot(a_bf16, hi) + _dot(a_bf16, mid)


def _log1pexp_neg_abs(x):
    return jnp.log(1.0 + jnp.exp(-jnp.abs(x)))


def _softplus(x):
    return jnp.maximum(x, 0.0) + _log1pexp_neg_abs(x)


def _mid_rows(x, half):
    n = x.shape[0]
    blk = 2 * half
    if blk >= 2 * SUBLANES:
        return jnp.concatenate(
            [jnp.broadcast_to(x[b0 + half - 1:b0 + half, :], (blk, x.shape[1])) for b0 in range(0, n, blk)], axis=0)
    x3 = x.reshape(n // SUBLANES, SUBLANES, x.shape[1])
    sub = lax.broadcasted_iota(jnp.int32, x3.shape, 1)
    out = None
    for b0 in range(0, SUBLANES, blk):
        piece = jnp.broadcast_to(x3[:, b0 + half - 1:b0 + half, :], x3.shape)
        out = piece if out is None else jnp.where(sub >= b0, piece, out)
    return out.reshape(x.shape)


def _silu(x):
    return x * jax.nn.sigmoid(x)


def _block_rows(blocks):
    n = len(blocks)
    rows = []
    for i, blk in enumerate(blocks):
        zero = jnp.zeros_like(blk)
        rows.append(jnp.concatenate([blk if j == i else zero for j in range(n)], axis=1))
    return jnp.concatenate(rows, axis=0)


def _run_round_robin(tasks):
    tasks = list(tasks)
    while tasks:
        for t in list(tasks):
            try:
                next(t)
            except StopIteration:
                tasks.remove(t)


def _layer_kernel(*refs, tile_rows, tile_seqs, d_model, depth, layer, emit_state, final_norm):
    (h_ref, normw_ref, win_ref, convw_ref, negexp_ref, dtb_ref, gna_ref, gnb_ref, hlb_ref,
     wba_ref, wbb_ref, wout_ref, fnw_ref, sa0_ref, sb0_ref, conv0_ref) = refs[:16]
    if emit_state:
        out_ref, sa_out_ref, sb_out_ref, conv_out_ref = refs[16:20]
        scratch = refs[20:]
    else:
        out_ref = refs[16]
        scratch = refs[17:]
    proj_ref, convbuf, sa_ref, sb_ref, ya_ref, yb_ref = scratch

    f32 = jnp.float32
    tt = tile_rows
    width_rest = proj_ref.shape[1]
    off_gb = OFF_GA + d_model
    off_sm = off_gb + d_model
    L = CHUNK
    n_chunks = tt // L
    ns = tile_seqs

    @pl.when(pl.program_id(1) == 0)
    def _():
        for sq in range(ns):
            sa_ref[sq] = sa0_ref[...]
            sb_ref[sq] = sb0_ref[...]
            convbuf[sq, 0:SUBLANES, :] = conv0_ref[...]

    h = h_ref[...].reshape(ns * tt, d_model)
    xn = h * lax.rsqrt(jnp.mean(h * h, axis=-1, keepdims=True) + EPS) * normw_ref[...]
    xn = _bf16(xn)
    qkv_raw = _dot(xn, win_ref[:, 0:QKV_W])
    for sq in range(ns):
        convbuf[sq, SUBLANES:SUBLANES + tt, :] = qkv_raw[sq * tt:(sq + 1) * tt]
    col = 0
    while col < width_rest:
        nxt = min(col + 1024, width_rest)
        proj_ref[:, col:nxt] = _dot(xn, win_ref[:, QKV_W + col:QKV_W + nxt])
        col = nxt

    hlb_rows = [hlb_ref[i:i + 1, :] for i in range(depth)]
    hmax = functools.reduce(jnp.maximum, hlb_rows)
    hexp = [jnp.exp(r - hmax) for r in hlb_rows]
    hsum = functools.reduce(lambda a, b: a + b, hexp)
    lb = jnp.zeros_like(hmax)
    for i in range(1, layer + 1):
        lb = lb + hexp[i] / hsum

    ci = lax.broadcasted_iota(jnp.int32, (L, CATW), 0)
    cj = lax.broadcasted_iota(jnp.int32, (L, CATW), 1) & (L - 1)
    causal_cat = ci >= cj
    strict_cat = ci > cj
    eye_cat = jnp.where(ci == cj, 1.0, 0.0).astype(f32)
    pair_cat = (ci & -PAIR) == (cj & -PAIR)
    bi = lax.broadcasted_iota(jnp.int32, (CATW, CATW), 0)
    bj = lax.broadcasted_iota(jnp.int32, (CATW, CATW), 1)
    bd_mask = (bi & -L) == (bj & -L)
    row = lax.broadcasted_iota(jnp.int32, (L, L), 0)
    colj = lax.broadcasted_iota(jnp.int32, (L, L), 1)
    tril_bf = _bf16(jnp.where(row >= colj, 1.0, 0.0))
    lower_left = {s: ((ci & -(2 * s)) == (cj & -(2 * s))) & ((ci & s) != 0) & ((cj & s) == 0)
                  for s in (PAIR, 2 * PAIR, 4 * PAIR)}
    same_block = {blk: (ci & -blk) == (cj & -blk) for blk in (2, 4, 8, 16, 32)}
    row128 = lax.broadcasted_iota(jnp.int32, (L, DH), 0)
    lane128 = lax.broadcasted_iota(jnp.int32, (L, DH), 1)
    convw = convw_ref[...]
    negexp = negexp_ref[...]
    dtb = dtb_ref[...]
    gna = gna_ref[...]
    gnb = gnb_ref[...]

    def block_diag(cat):
        return _bf16(jnp.where(bd_mask, jnp.concatenate([cat] * HEADS, axis=0), 0.0))

    def conv_silu(sq, r0, c0):
        acc = convbuf[sq, r0 + SUBLANES:r0 + SUBLANES + L, c0:c0 + DH] * convw[CONV_K - 1:CONV_K, c0:c0 + DH]
        for j in range(1, CONV_K):
            acc = acc + (convbuf[sq, r0 + SUBLANES - j:r0 + SUBLANES - j + L, c0:c0 + DH]
                         * convw[CONV_K - 1 - j:CONV_K - j, c0:c0 + DH])
        return _silu(acc)

    gdn = [[None] * n_chunks for _ in range(ns)]
    hgrn = [[None] * n_chunks for _ in range(ns)]

    def gdn_prep(sq, c):
        r0 = sq * tt + c * L
        small = proj_ref[r0:r0 + L, off_sm:off_sm + DH]
        beta_all = jax.nn.sigmoid(small)
        g_all = (negexp * LOG2E) * _softplus(small + dtb)
        beta_b = [jnp.broadcast_to(beta_all[:, hd:hd + 1], (L, DH)) for hd in range(HEADS)]
        g_b = [jnp.broadcast_to(g_all[:, HEADS + hd:HEADS + hd + 1], (L, DH)) for hd in range(HEADS)]
        g_cat = jnp.concatenate([jnp.where(lane128 < L, g_b[2 * t], g_b[2 * t + 1]) for t in range(HEADS // 2)],
                                axis=1)
        dec = _dot_split_rhs(tril_bf, jnp.concatenate([jnp.where(strict_cat, g_cat, 0.0)] + g_b, axis=1))
        q, k, v = [], [], []
        for hd in range(HEADS):
            qc = conv_silu(sq, c * L, hd * DH)
            kc = conv_silu(sq, c * L, WK + hd * DH)
            v.append(conv_silu(sq, c * L, 2 * WK + hd * DH))
            q.append(qc * lax.rsqrt(jnp.sum(qc * qc, axis=-1, keepdims=True) + EPS) * (DH ** -0.5))
            k.append(kc * lax.rsqrt(jnp.sum(kc * kc, axis=-1, keepdims=True) + EPS))
        yield
        decay = jnp.exp2(dec[:, :CATW])
        gb = [dec[:, CATW + hd * DH:CATW + (hd + 1) * DH] for hd in range(HEADS)]
        kb = [k[hd] * beta_b[hd] for hd in range(HEADS)]
        k_bf = [_bf16(x) for x in k]
        lhs = jnp.concatenate([jnp.concatenate([_bf16(x) for x in kb], axis=1),
                               jnp.concatenate([_bf16(x) for x in q], axis=1)], axis=0)
        kkqk = _dot(lhs, _block_rows(k_bf), _NT)
        yield
        lcat = jnp.where(strict_cat, kkqk[:L] * decay, 0.0)
        scores = _bf16(jnp.where(causal_cat, kkqk[L:] * decay, 0.0))
        xb = jnp.where(pair_cat, -lcat, 0.0)
        x2 = _dot(_bf16(xb), block_diag(xb))
        yield
        bd2 = block_diag(x2)
        ipx = eye_cat + xb
        x4 = _dot(_bf16(x2), bd2)
        p1 = ipx + _dot(_bf16(ipx), bd2)
        yield
        inv = p1 + _dot(_bf16(p1), block_diag(x4))
        yield
        s = PAIR
        while s < L:
            e = _dot(_bf16(jnp.where(lower_left[s], lcat, 0.0)), block_diag(inv))
            yield
            inv = inv - _dot(_bf16(inv), block_diag(e))
            yield
            s *= 2
        eg = [jnp.exp2(x) for x in gb]
        rhs = [_bf16(jnp.concatenate([v[hd] * beta_b[hd], kb[hd] * eg[hd]], axis=1)) for hd in range(HEADS)]
        sol = _dot(_bf16(inv), _block_rows(rhs))
        yield
        g_last = [x[L - 1:L, :] for x in gb]
        gdn[sq][c] = dict(
            u=[sol[:, 2 * hd * DH:(2 * hd + 1) * DH] for hd in range(HEADS)],
            wq=[_bf16(jnp.concatenate(
                [jnp.concatenate([sol[:, (2 * hd + 1) * DH:(2 * hd + 2) * DH], q[hd] * eg[hd]], axis=0)
                 for hd in (2 * p, 2 * p + 1)], axis=1)) for p in range(HEADS // 2)],
            kr=[_bf16(jnp.concatenate([k[hd] * jnp.exp2(g_last[hd] - gb[hd]) for hd in (2 * p, 2 * p + 1)], axis=0))
                for p in range(HEADS // 2)],
            eg_last=[jnp.exp2(x) for x in g_last],
            scores=scores,
        )

    def hgrn_prep(sq, c):
        r0 = sq * tt + c * L
        q, k, v, log_f = [], [], [], []
        for hd in range(HEADS):
            cs = slice(hd * DH, (hd + 1) * DH)
            bq = proj_ref[r0:r0 + L, OFF_BQ + hd * DH:OFF_BQ + (hd + 1) * DH]
            zf = proj_ref[r0:r0 + L, OFF_BF + hd * DH:OFF_BF + (hd + 1) * DH]
            v.append(proj_ref[r0:r0 + L, OFF_BI + hd * DH:OFF_BI + (hd + 1) * DH])
            q.append(_silu(bq) * (DH ** -0.5))
            log_sig = -_softplus(-zf)
            if layer == 0:
                lf = log_sig
                k.append(jax.nn.sigmoid(-zf))
            else:
                lbh = lb[:, cs]
                pos = lbh > 0.0
                la = jnp.log(jnp.where(pos, lbh, 1.0))
                lc = jnp.log1p(-lbh) + log_sig
                lae = jnp.maximum(la, lc) + _log1pexp_neg_abs(la - lc)
                lf = jnp.where(pos, lae, log_sig)
                k.append(jnp.where(pos, 1.0 - lbh, 1.0) * jax.nn.sigmoid(-zf))
            log_f.append(lf * LOG2E)
        bc_all = _dot_split_rhs(tril_bf, jnp.concatenate(log_f, axis=1))
        yield
        bc = [bc_all[:, hd * DH:(hd + 1) * DH] for hd in range(HEADS)]
        o = [jnp.sum(q[hd] * k[hd], axis=-1, keepdims=True) * v[hd] for hd in range(HEADS)]
        pmat = None
        half = 1
        while half < L:
            blk = 2 * half
            second = (row128 & (blk - 1)) >= half
            qs, ks = [], []
            for hd in range(HEADS):
                zd = jnp.exp2(-jnp.abs(bc[hd] - _mid_rows(bc[hd], half)))
                qs.append(_bf16(jnp.where(second, q[hd] * zd, 0.0)))
                ks.append(_bf16(jnp.where(second, 0.0, k[hd] * zd)))
            pl_ = _dot(jnp.concatenate(qs, axis=1), _block_rows(ks), _NT)
            if blk < L:
                pl_ = jnp.where(same_block[blk], pl_, 0.0)
            pmat = pl_ if pmat is None else pmat + pl_
            yield
            half = blk
        v_bf = [_bf16(x) for x in v]
        o_intra = _dot(_bf16(pmat), _block_rows(v_bf))
        yield
        bc_last = [x[L - 1:L, :] for x in bc]
        hgrn[sq][c] = dict(
            o=[o[hd] + o_intra[:, hd * DH:(hd + 1) * DH] for hd in range(HEADS)],
            qd=[_bf16(jnp.concatenate([q[hd] * jnp.exp2(bc[hd]) for hd in (2 * p, 2 * p + 1)], axis=1))
                for p in range(HEADS // 2)],
            kd=[_bf16(k[hd] * jnp.exp2(bc_last[hd] - bc[hd])) for hd in range(HEADS)],
            v2=[jnp.concatenate(v_bf[2 * p:2 * p + 2], axis=0) for p in range(HEADS // 2)],
            e_last=[jnp.exp2(x) for x in bc_last],
        )

    _run_round_robin([f(sq, c) for f in (gdn_prep, hgrn_prep) for sq in range(ns) for c in range(n_chunks)])

    def gdn_seq(sq):
        s = [sa_ref[sq, hd] for hd in range(HEADS)]
        for c in range(n_chunks):
            r0 = sq * tt + c * L
            g = gdn[sq][c]
            s_bf = [_bf16(x) for x in s]
            ws_pair = [_dot(g["wq"][p], _block_rows(s_bf[2 * p:2 * p + 2])) for p in range(HEADS // 2)]
            ws = [ws_pair[hd // 2][:, (hd % 2) * DH:(hd % 2 + 1) * DH] for hd in range(HEADS)]
            yield
            v_new = [_bf16(g["u"][hd] - ws[hd][:L]) for hd in range(HEADS)]
            kv = [_dot(g["kr"][p], _block_rows(v_new[2 * p:2 * p + 2]), _TN) for p in range(HEADS // 2)]
            s = [s[hd] * g["eg_last"][hd] + kv[hd // 2][:, (hd % 2) * DH:(hd % 2 + 1) * DH] for hd in range(HEADS)]
            sv = _dot(g["scores"], _block_rows(v_new))
            yield
            for hd in range(HEADS):
                o = ws[hd][L:] + sv[:, hd * DH:(hd + 1) * DH]
                z = proj_ref[r0:r0 + L, OFF_AZ + hd * DH:OFF_AZ + (hd + 1) * DH]
                y = o * lax.rsqrt(jnp.mean(o * o, axis=-1, keepdims=True) + EPS) * gna * _silu(z)
                ya_ref[r0:r0 + L, hd * DH:(hd + 1) * DH] = _bf16(y)
        for hd in range(HEADS):
            sa_ref[sq, hd] = s[hd]

    def hgrn_seq(sq):
        st = [sb_ref[sq, hd] for hd in range(HEADS)]
        for c in range(n_chunks):
            r0 = sq * tt + c * L
            g = hgrn[sq][c]
            st_bf = [_bf16(x) for x in st]
            o_pair = [_dot(g["qd"][p], _block_rows(st_bf[2 * p:2 * p + 2]), _NT) for p in range(HEADS // 2)]
            o_inter = [o_pair[hd // 2][:, (hd % 2) * DH:(hd % 2 + 1) * DH] for hd in range(HEADS)]
            vk = [_dot(g["v2"][p], _block_rows(g["kd"][2 * p:2 * p + 2]), _TN) for p in range(HEADS // 2)]
            st = [st[hd] * g["e_last"][hd] + vk[hd // 2][:, (hd % 2) * DH:(hd % 2 + 1) * DH] for hd in range(HEADS)]
            yield
            for hd in range(HEADS):
                o = g["o"][hd] + o_inter[hd]
                zg = proj_ref[r0:r0 + L, OFF_BG + hd * DH:OFF_BG + (hd + 1) * DH]
                y = o * lax.rsqrt(jnp.mean(o * o, axis=-1, keepdims=True) + EPS) * gnb * _silu(zg)
                yb_ref[r0:r0 + L, hd * DH:(hd + 1) * DH] = _bf16(y)
            yield
        for hd in range(HEADS):
            sb_ref[sq, hd] = st[hd]

    _run_round_robin([f(sq) for f in (gdn_seq, hgrn_seq) for sq in range(ns)])

    for sq in range(ns):
        convbuf[sq, 0:SUBLANES, :] = convbuf[sq, tt:tt + SUBLANES, :]

    ya = _dot(ya_ref[...], wba_ref[...])
    yb = _dot(yb_ref[...], wbb_ref[...])
    mixed = (jax.nn.sigmoid(proj_ref[:, OFF_GA:OFF_GA + d_model]) * ya
             + jax.nn.sigmoid(proj_ref[:, off_gb:off_gb + d_model]) * yb)
    out = h_ref[...].reshape(ns * tt, d_model) + _dot(_bf16(mixed), wout_ref[...])
    if final_norm:
        out = out * lax.rsqrt(jnp.mean(out * out, axis=-1, keepdims=True) + EPS) * fnw_ref[...]
    out_ref[...] = out.reshape(ns, tt, d_model)

    if emit_state:
        sa_out_ref[...] = sa_ref[0]
        sb_out_ref[...] = sb_ref[0]
        conv_out_ref[...] = convbuf[0, 0:SUBLANES, :]


def _layer(h, p, layer, depth, states, *, emit_state, final_norm):
    b, t, d = h.shape
    tt = min(t, MAX_TILE_ROWS)
    ns = min(b, MAX_TILE_SEQS)
    assert t % tt == 0 and tt % CHUNK == 0 and b % ns == 0
    width_rest = p["w_in"].shape[1] - QKV_W
    f32 = jnp.float32

    def const(shape):
        return pl.BlockSpec(shape, lambda i, j: (0,) * len(shape), pipeline_mode=pl.Buffered(1))

    in_specs = [
        pl.BlockSpec((ns, tt, d), lambda i, j: (i, j, 0)),
        const((1, d)),
        const(p["w_in"].shape),
        const((SUBLANES, QKV_W)),
        const((1, DH)), const((1, DH)), const((1, DH)), const((1, DH)),
        const((depth, WK)),
        const((WK, d)), const((WK, d)), const((d, d)),
        const((1, d)),
        const((HEADS, DH, DH)), const((HEADS, DH, DH)), const((SUBLANES, QKV_W)),
    ]
    out_shape = [jax.ShapeDtypeStruct((b, t, d), f32)]
    out_specs = [pl.BlockSpec((ns, tt, d), lambda i, j: (i, j, 0))]
    if emit_state:
        assert b == 1 and t == tt
        out_shape += [jax.ShapeDtypeStruct((HEADS, DH, DH), f32), jax.ShapeDtypeStruct((HEADS, DH, DH), f32),
                      jax.ShapeDtypeStruct((SUBLANES, QKV_W), f32)]
        out_specs += [const((HEADS, DH, DH)), const((HEADS, DH, DH)), const((SUBLANES, QKV_W))]
    scratch = [
        pltpu.VMEM((ns * tt, width_rest), f32),
        pltpu.VMEM((ns, tt + SUBLANES, QKV_W), f32),
        pltpu.VMEM((ns, HEADS, DH, DH), f32),
        pltpu.VMEM((ns, HEADS, DH, DH), f32),
        pltpu.VMEM((ns * tt, WK), MXU_DTYPE),
        pltpu.VMEM((ns * tt, WK), MXU_DTYPE),
    ]
    kern = functools.partial(_layer_kernel, tile_rows=tt, tile_seqs=ns, d_model=d, depth=depth, layer=layer,
                             emit_state=emit_state, final_norm=final_norm)
    outs = pl.pallas_call(
        kern,
        grid=(b // ns, t // tt),
        in_specs=in_specs,
        out_specs=out_specs,
        out_shape=out_shape,
        scratch_shapes=scratch,
        compiler_params=pltpu.CompilerParams(
            dimension_semantics=("arbitrary", "arbitrary"),
            vmem_limit_bytes=56 * 1024 * 1024),
        name=f"layer{layer}_{'meta' if emit_state else 'main'}",
    )(h, p["norm_w"], p["w_in"], p["conv_w"], p["negexp"], p["dtb"], p["gna"], p["gnb"], p["hlb"],
      p["wba"], p["wbb"], p["wout"], p["fnw"], *states)
    return outs


def _prep_layer(l, norm_w, w_in, conv_w, a_log, dt_bias, gnorm_a, gnorm_b, hlb, w_branch_a, w_branch_b,
                w_out, final_norm_w):
    d = w_in.shape[1]
    f32 = jnp.float32
    wl = w_in[l]
    o = [0]
    for s in (WK, WK, WK, HEADS, HEADS, WK, WK, WK, WK, WK, d, d):
        o.append(o[-1] + s)
    small = jnp.pad(wl[:, o[3]:o[5]], ((0, 0), (0, DH - 2 * HEADS)))
    w_re = jnp.concatenate([wl[:, o[0]:o[3]], wl[:, o[5]:o[12]], small], axis=1)
    lane_pad = (HEADS, DH - 2 * HEADS)
    return {
        "norm_w": norm_w[l][None].astype(f32),
        "w_in": w_re.astype(MXU_DTYPE),
        "conv_w": jnp.pad(conv_w[l].astype(f32), ((0, SUBLANES - CONV_K), (0, 0))),
        "negexp": jnp.pad(-jnp.exp(a_log[l].astype(f32)), lane_pad)[None],
        "dtb": jnp.pad(dt_bias[l].astype(f32), lane_pad)[None],
        "gna": gnorm_a[l][None].astype(f32),
        "gnb": gnorm_b[l][None].astype(f32),
        "hlb": hlb.astype(f32),
        "wba": w_branch_a[l].astype(MXU_DTYPE),
        "wbb": w_branch_b[l].astype(MXU_DTYPE),
        "wout": w_out[l].astype(MXU_DTYPE),
        "fnw": final_norm_w[None].astype(f32),
    }


def kernel(x, meta_tokens, norm_w, w_in, conv_w, a_log, dt_bias, gnorm_a, gnorm_b, hgrn_lower_bounds,
           w_branch_a, w_branch_b, w_out, final_norm_w):
    depth = w_in.shape[0]
    f32 = jnp.float32
    assert N_META <= CHUNK
    h = x
    hm = jnp.pad(meta_tokens.astype(x.dtype), ((CHUNK - N_META, 0), (0, 0)))[None]
    zero_state = (jnp.zeros((HEADS, DH, DH), f32), jnp.zeros((HEADS, DH, DH), f32),
                  jnp.zeros((SUBLANES, QKV_W), f32))
    for l in range(depth):
        p = _prep_layer(l, norm_w, w_in, conv_w, a_log, dt_bias, gnorm_a, gnorm_b, hgrn_lower_bounds,
                        w_branch_a, w_branch_b, w_out, final_norm_w)
        hm, s_a, s_b, conv_hist = _layer(hm, p, l, depth, zero_state, emit_state=True, final_norm=False)
        (h,) = _layer(h, p, l, depth, (s_a, s_b, conv_hist), emit_state=False, final_norm=(l == depth - 1))
    return h
```
